```python
import jax, jax.numpy as jnp
from jax import lax
import numpy as np

D_MODEL = 2048
BATCH = 8
SEQ = 2048
DEPTH = 1

GDN_HEADS = 8
GDN_HEAD_DIM = 128
GDN_WIDTH = GDN_HEADS * GDN_HEAD_DIM
CONV_WIDTH = 4
GDN_CHUNK = 64
MOBA_HEADS = 8
MOBA_HEAD_DIM = 128
MOBA_WIDTH = MOBA_HEADS * MOBA_HEAD_DIM
MOBA_BLOCK = 256
MOBA_TOPK = 3
MOBA_QCHUNK = 128

NORM_EPS = 1e-6
NEG_INF = -1e30
SPLITS = (3 * GDN_WIDTH, GDN_WIDTH, GDN_HEADS, GDN_HEADS, 3 * MOBA_WIDTH, MOBA_WIDTH, D_MODEL, D_MODEL)
IN_WIDTH = 3 * GDN_WIDTH + GDN_WIDTH + 2 * GDN_HEADS + 3 * MOBA_WIDTH + MOBA_WIDTH + 2 * D_MODEL

kernel_name = "hybrid_gdn_moba_gated_merge"


def rms_norm(x, w):
    xf = x.astype(jnp.float32)
    y = xf * lax.rsqrt(jnp.mean(xf * xf, axis=-1, keepdims=True) + NORM_EPS)
    return (y * w.astype(jnp.float32)).astype(x.dtype)


def l2_normalize(x):
    return x * lax.rsqrt(jnp.sum(x * x, axis=-1, keepdims=True) + NORM_EPS)


def causal_depthwise_conv(x, w):
    K = w.shape[0]
    S = x.shape[1]
    xp = jnp.pad(x, ((0, 0), (K - 1, 0), (0, 0)))
    out = w[0] * xp[:, 0:S]
    for j in range(1, K):
        out = out + w[j] * xp[:, j:j + S]
    return out


def gated_delta_rule(q, k, v, beta, g):
    B, H, S, dk = q.shape
    dv = v.shape[-1]
    C = GDN_CHUNK
    N = S // C
    q = q * (dk ** -0.5)
    q = q.reshape(B, H, N, C, dk)
    k = k.reshape(B, H, N, C, dk)
    v = v.reshape(B, H, N, C, dv)
    beta = beta.reshape(B, H, N, C)
    g = jnp.cumsum(g.reshape(B, H, N, C), axis=-1)
    kb = k * beta[..., None]
    vb = v * beta[..., None]
    incl = jnp.tril(jnp.ones((C, C), dtype=bool))
    strict = jnp.tril(jnp.ones((C, C), dtype=bool), -1)
    decay = jnp.exp(jnp.where(incl, g[..., :, None] - g[..., None, :], NEG_INF))
    L = jnp.where(strict, jnp.einsum('bhnid,bhnjd->bhnij', kb, k) * decay, 0.0)
    eye = jnp.eye(C, dtype=jnp.float32)
    T = lax.linalg.triangular_solve(L + eye, jnp.broadcast_to(eye, L.shape), left_side=True, lower=True)
    u = jnp.einsum('bhnij,bhnje->bhnie', T, vb)
    w = jnp.einsum('bhnij,bhnjd->bhnid', T, kb * jnp.exp(g)[..., None])
    intra = jnp.where(incl, jnp.einsum('bhnid,bhnjd->bhnij', q, k) * decay, 0.0)
    qg = q * jnp.exp(g)[..., None]
    kdec = k * jnp.exp(g[..., -1:] - g)[..., None]
    g_last = jnp.exp(g[..., -1])
    xs = (jnp.moveaxis(qg, 2, 0), jnp.moveaxis(kdec, 2, 0), jnp.moveaxis(u, 2, 0),
          jnp.moveaxis(w, 2, 0), jnp.moveaxis(intra, 2, 0), jnp.moveaxis(g_last, 2, 0))

    def step(state, inp):
        qg_i, kd_i, u_i, w_i, a_i, gl_i = inp
        v_new = u_i - jnp.einsum('bhcd,bhde->bhce', w_i, state)
        o = jnp.einsum('bhcd,bhde->bhce', qg_i, state) + jnp.einsum('bhij,bhje->bhie', a_i, v_new)
        state = state * gl_i[..., None, None] + jnp.einsum('bhcd,bhce->bhde', kd_i, v_new)
        return state, o

    state0 = jnp.zeros((B, H, dk, dv), jnp.float32)
    _, o = lax.scan(step, state0, xs)
    return jnp.moveaxis(o, 0, 2).reshape(B, H, S, dv)


def moba_attention(q, k, v):
    B, S, H, d = q.shape
    NB = -(-S // MOBA_BLOCK)
    S_pad = NB * MOBA_BLOCK
    QC = MOBA_QCHUNK
    NQ = S // QC
    k_sel = min(MOBA_TOPK, NB)
    pad = ((0, 0), (0, S_pad - S), (0, 0), (0, 0))
    kp = jnp.pad(k, pad).reshape(B, NB, MOBA_BLOCK, H, d).transpose(0, 3, 1, 2, 4)
    vp = jnp.pad(v, pad).reshape(B, NB, MOBA_BLOCK, H, d).transpose(0, 3, 1, 2, 4)
    kmean = jnp.mean(kp.astype(jnp.float32), axis=3)
    qh = q.transpose(0, 2, 1, 3)
    gate = jnp.einsum('bhsd,bhnd->bhsn', qh.astype(jnp.float32), kmean)
    q_blk = jnp.arange(S) // MOBA_BLOCK
    fully_past = jnp.arange(NB)[None, :] < q_blk[:, None]
    gate = jnp.where(fully_past, gate, NEG_INF)
    _, idx = lax.top_k(gate, k_sel)

    q_steps = qh.reshape(B, H, NQ, QC, d).transpose(0, 2, 1, 3, 4).reshape(B * NQ, H, QC, d)
    idx_steps = idx.reshape(B, H, NQ, QC, k_sel).transpose(0, 2, 1, 3, 4).reshape(B * NQ, H, QC, k_sel)
    b_ids = jnp.repeat(jnp.arange(B, dtype=jnp.int32), NQ)
    c_ids = jnp.tile(jnp.arange(NQ, dtype=jnp.int32), B)
    scale = MOBA_HEAD_DIM ** -0.5
    hh = jnp.arange(H)[:, None, None]

    def step(args):
        qc, ic, b, c = args
        kb = kp[b]
        vb = vp[b]
        kg = kb[hh, ic]
        vg = vb[hh, ic]
        own = (c * QC) // MOBA_BLOCK
        ko = lax.dynamic_index_in_dim(kb, own, axis=1, keepdims=False)
        vo = lax.dynamic_index_in_dim(vb, own, axis=1, keepdims=False)
        s_past = jnp.einsum('hqd,hqjpd->hqjp', qc, kg, preferred_element_type=jnp.float32)
        s_past = s_past.reshape(H, QC, k_sel * MOBA_BLOCK)
        s_own = jnp.einsum('hqd,hpd->hqp', qc, ko, preferred_element_type=jnp.float32)
        t_abs = c * QC + jnp.arange(QC)
        slot_ok = jnp.arange(k_sel)[None, :] < (t_abs // MOBA_BLOCK)[:, None]
        slot_ok = jnp.broadcast_to(slot_ok[:, :, None], (QC, k_sel, MOBA_BLOCK)).reshape(QC, k_sel * MOBA_BLOCK)
        own_ok = (own * MOBA_BLOCK + jnp.arange(MOBA_BLOCK))[None, :] <= t_abs[:, None]
        mask = jnp.concatenate([slot_ok, own_ok], axis=-1)[None]
        s = jnp.concatenate([s_past, s_own], axis=-1) * scale
        p = jax.nn.softmax(jnp.where(mask, s, NEG_INF), axis=-1).astype(vg.dtype)
        p_past = p[..., :k_sel * MOBA_BLOCK].reshape(H, QC, k_sel, MOBA_BLOCK)
        p_own = p[..., k_sel * MOBA_BLOCK:]
        return (jnp.einsum('hqjp,hqjpd->hqd', p_past, vg)
                + jnp.einsum('hqp,hpd->hqd', p_own, vo))

    o = lax.map(step, (q_steps, idx_steps, b_ids, c_ids))
    return o.reshape(B, NQ, H, QC, d).transpose(0, 1, 3, 2, 4).reshape(B, S, H * d)


def hybrid_mixer(h, w_in, conv_w, a_log, dt_bias, gdn_norm_w, w_branch_a, w_branch_b, w_out):
    B, S, _ = h.shape
    proj = h @ w_in
    offsets = []
    acc = 0
    for n in SPLITS[:-1]:
        acc += n
        offsets.append(acc)
    gdn_qkv, gdn_z, gdn_b, gdn_a, moba_qkv, moba_z, gate_a, gate_b = jnp.split(proj, offsets, axis=-1)

    qkv = jax.nn.silu(causal_depthwise_conv(gdn_qkv, conv_w)).astype(jnp.float32)
    qa, ka, va = jnp.split(qkv, 3, axis=-1)
    to_heads = lambda t: t.reshape(B, S, GDN_HEADS, GDN_HEAD_DIM).transpose(0, 2, 1, 3)
    qa, ka, va = l2_normalize(to_heads(qa)), l2_normalize(to_heads(ka)), to_heads(va)
    beta = jax.nn.sigmoid(gdn_b.astype(jnp.float32)).transpose(0, 2, 1)
    g = (-jnp.exp(a_log.astype(jnp.float32))
         * jax.nn.softplus(gdn_a.astype(jnp.float32) + dt_bias.astype(jnp.float32))).transpose(0, 2, 1)
    oa = gated_delta_rule(qa, ka, va, beta, g).transpose(0, 2, 1, 3)
    oa = oa * lax.rsqrt(jnp.mean(oa * oa, axis=-1, keepdims=True) + NORM_EPS)
    za = gdn_z.astype(jnp.float32).reshape(B, S, GDN_HEADS, GDN_HEAD_DIM)
    oa = (oa * gdn_norm_w.astype(jnp.float32) * jax.nn.silu(za)).reshape(B, S, GDN_WIDTH).astype(h.dtype)
    u_a = oa @ w_branch_a

    qb, kb, vb = jnp.split(moba_qkv, 3, axis=-1)
    to_mh = lambda t: t.reshape(B, S, MOBA_HEADS, MOBA_HEAD_DIM)
    ob = moba_attention(to_mh(qb), to_mh(kb), to_mh(vb)).astype(h.dtype)
    ob = ob * jax.nn.silu(moba_z)
    u_b = ob @ w_branch_b

    merged = jax.nn.sigmoid(gate_a) * u_a + jax.nn.sigmoid(gate_b) * u_b
    return merged @ w_out


def setup_inputs(seed: int = 0) -> dict:
    key = jax.random.key(seed)
    ks = jax.random.split(key, 12)
    f32 = jnp.float32
    x = jax.random.normal(ks[0], (BATCH, SEQ, D_MODEL), f32)
    pre_norm_w = 1.0 + 0.05 * jax.random.normal(ks[1], (DEPTH, D_MODEL), f32)
    w_in = jax.random.normal(ks[2], (DEPTH, D_MODEL, IN_WIDTH), f32) * (D_MODEL ** -0.5)
    conv_w = jax.random.normal(ks[3], (DEPTH, CONV_WIDTH, 3 * GDN_WIDTH), f32) * (CONV_WIDTH ** -0.5)
    a_log = jnp.log(jax.random.uniform(ks[4], (DEPTH, GDN_HEADS), f32, minval=1.0, maxval=16.0))
    dt = jnp.exp(jax.random.uniform(ks[5], (DEPTH, GDN_HEADS), f32, minval=np.log(1e-3), maxval=np.log(1e-1)))
    dt_bias = dt + jnp.log(-jnp.expm1(-dt))
    gdn_norm_w = 1.0 + 0.05 * jax.random.normal(ks[6], (DEPTH, GDN_HEAD_DIM), f32)
    w_branch_a = jax.random.normal(ks[7], (DEPTH, GDN_WIDTH, D_MODEL), f32) * (GDN_WIDTH ** -0.5)
    w_branch_b = jax.random.normal(ks[8], (DEPTH, MOBA_WIDTH, D_MODEL), f32) * (MOBA_WIDTH ** -0.5)
    w_out = jax.random.normal(ks[9], (DEPTH, D_MODEL, D_MODEL), f32) * (D_MODEL ** -0.5)
    post_norm_w = 1.0 + 0.05 * jax.random.normal(ks[10], (DEPTH, D_MODEL), f32)
    return {"x": x, "pre_norm_w": pre_norm_w, "w_in": w_in, "conv_w": conv_w, "a_log": a_log,
            "dt_bias": dt_bias, "gdn_norm_w": gdn_norm_w, "w_branch_a": w_branch_a,
            "w_branch_b": w_branch_b, "w_out": w_out, "post_norm_w": post_norm_w}


def reference(x, pre_norm_w, w_in, conv_w, a_log, dt_bias, gdn_norm_w, w_branch_a, w_branch_b, w_out, post_norm_w):
    for l in range(DEPTH):
        h = rms_norm(x, pre_norm_w[l])
        y = hybrid_mixer(h, w_in[l], conv_w[l], a_log[l], dt_bias[l], gdn_norm_w[l],
                         w_branch_a[l], w_branch_b[l], w_out[l])
        x = x + rms_norm(y, post_norm_w[l])
    return x
```

```python
import functools

import jax
import jax.numpy as jnp
from jax import lax
from jax.experimental import pallas as pl
from jax.experimental.pallas import tpu as pltpu

F32 = jnp.float32
BF16 = jnp.bfloat16

NORM_EPS = 1e-6
NEG_INF = -1e30

HEADS = 8
HEAD_DIM = 128
CONV_WIDTH = 4
MOBA_BLOCK = 256
MOBA_TOPK = 3
GDN_CHUNK = 128
GDN_UNROLL = 2

LANES = 128
SUBLANES = 8

QKV_A, Z_A, QKV_B, Z_B = 0, 24, 32, 56
GATE_A_BLK2048, GATE_B_BLK2048 = 4, 5

INPROJ_TM = 1024
INPROJ_TN = 1024
OUT_TM = 256

VMEM_LIMIT = 56 * 1024 * 1024


def _dot(a, b):
    return jnp.dot(a, b, preferred_element_type=F32)


def _dot_nt(a, b):
    return lax.dot_general(a, b, (((1,), (1,)), ((), ())), preferred_element_type=F32)


def _dot_tn(a, b):
    return lax.dot_general(a, b, (((0,), (0,)), ((), ())), preferred_element_type=F32)


def _sigmoid(x):
    return 1.0 / (1.0 + jnp.exp(-x))


def _softplus(x):
    return jnp.maximum(x, 0.0) + jnp.log1p(jnp.exp(-jnp.abs(x)))


def _inproj_kernel(x_ref, nw_ref, w_ref, wba_ref, alog_ref, dtb_ref, proj_ref, bg_ref, h_ref):
    j = pl.program_id(1)

    @pl.when(j == 0)
    def _():
        xf = x_ref[...]
        inv = lax.rsqrt(jnp.mean(xf * xf, axis=-1, keepdims=True) + NORM_EPS)
        h_ref[...] = (xf * inv * nw_ref[...]).astype(BF16)
        ba = _dot(h_ref[...], wba_ref[...])
        lane = lax.broadcasted_iota(jnp.int32, ba.shape, 1)
        beta = _sigmoid(ba)
        g = -jnp.exp(alog_ref[...]) * _softplus(ba + dtb_ref[...])
        bg_ref[...] = jnp.where(lane < HEADS, beta, g)

    acc = _dot(h_ref[...], w_ref[...])
    sig = _sigmoid(acc)
    is_silu = jnp.logical_or(j == 3, j == 7)
    is_sig = j >= 8
    out = jnp.where(is_sig, sig, acc * jnp.where(is_silu, sig, 1.0))
    proj_ref[...] = out.astype(BF16)


def _inproj(x2, nw, w_main, w_ba, alog_row, dtb_row):
    m, d = x2.shape
    n = w_main.shape[1]
    tm, tn = INPROJ_TM, INPROJ_TN
    return pl.pallas_call(
        _inproj_kernel,
        grid=(m // tm, n // tn),
        in_specs=[
            pl.BlockSpec((tm, d), lambda i, j: (i, 0)),
            pl.BlockSpec((1, d), lambda i, j: (0, 0)),
            pl.BlockSpec((d, tn), lambda i, j: (0, j)),
            pl.BlockSpec((d, LANES), lambda i, j: (0, 0)),
            pl.BlockSpec((1, LANES), lambda i, j: (0, 0)),
            pl.BlockSpec((1, LANES), lambda i, j: (0, 0)),
        ],
        out_specs=[
            pl.BlockSpec((tm, tn), lambda i, j: (i, j)),
            pl.BlockSpec((tm, LANES), lambda i, j: (i, 0)),
        ],
        out_shape=[
            jax.ShapeDtypeStruct((m, n), BF16),
            jax.ShapeDtypeStruct((m, LANES), F32),
        ],
        scratch_shapes=[pltpu.VMEM((tm, d), BF16)],
        compiler_params=pltpu.CompilerParams(
            dimension_semantics=("parallel", "arbitrary"), vmem_limit_bytes=VMEM_LIMIT),
        name="inproj",
    )(x2, nw, w_main, w_ba, alog_row, dtb_row)


def _tri_inverse(lmat, row, col):
    c = lmat.shape[0]

    def sub_mask(s_log):
        bi = jnp.right_shift(row, s_log)
        bj = jnp.right_shift(col, s_log)
        return jnp.logical_and(jnp.bitwise_xor(bi, bj) == 1, jnp.bitwise_and(bi, 1) == 1)

    eye = (row == col).astype(F32)
    td = eye - jnp.where(sub_mask(0), lmat, 0.0)
    s_log = 1
    while (1 << s_log) < c:
        cs = jnp.where(sub_mask(s_log), lmat, 0.0).astype(BF16)
        tb = td.astype(BF16)
        td = td - _dot(_dot(tb, cs).astype(BF16), tb)
        s_log += 1
    return td


def _gdn_kernel(q_ref, k_ref, v_ref, cwq_ref, cwk_ref, cwv_ref, bg_ref, zs_ref, nw_ref, o_ref,
                xpad, qn, kn, vn, betab, gb, xs, cs, qps, olocs, gls):
    s = q_ref.shape[0]
    c = GDN_CHUNK
    n_chunks = s // c
    h = pl.program_id(1)

    def conv_silu(src_ref, cw_ref):
        xpad[0:SUBLANES, :] = jnp.zeros((SUBLANES, HEAD_DIM), F32)
        xpad[SUBLANES:SUBLANES + s, :] = src_ref[...].astype(F32)
        base = SUBLANES - (CONV_WIDTH - 1)
        acc = cw_ref[0:1, :] * xpad[base:base + s, :]
        for j in range(1, CONV_WIDTH):
            acc = acc + cw_ref[j:j + 1, :] * xpad[base + j:base + j + s, :]
        return acc * _sigmoid(acc)

    def l2n(x):
        return x * lax.rsqrt(jnp.sum(x * x, axis=-1, keepdims=True) + NORM_EPS)

    qn[...] = l2n(conv_silu(q_ref, cwq_ref)) * (HEAD_DIM ** -0.5)
    kn[...] = l2n(conv_silu(k_ref, cwk_ref))
    vn[...] = conv_silu(v_ref, cwv_ref)

    bg = bg_ref[...]
    lane = lax.broadcasted_iota(jnp.int32, bg.shape, 1)
    beta_col = jnp.sum(jnp.where(lane == h, bg, 0.0), axis=-1, keepdims=True)
    g_col = jnp.sum(jnp.where(lane == h + HEADS, bg, 0.0), axis=-1, keepdims=True)
    betab[...] = jnp.broadcast_to(beta_col, bg.shape)
    gb[...] = jnp.broadcast_to(g_col, bg.shape)

    row = lax.broadcasted_iota(jnp.int32, (c, c), 0)
    col = lax.broadcasted_iota(jnp.int32, (c, c), 1)
    incl = row >= col
    strict = row > col
    tril_b = incl.astype(BF16)

    def cumsum_rows(x):
        x1 = x.astype(BF16)
        r1 = x - x1.astype(F32)
        x2 = r1.astype(BF16)
        x3 = (r1 - x2.astype(F32)).astype(BF16)
        return _dot(tril_b, x1) + _dot(tril_b, x2) + _dot(tril_b, x3)

    def chunk_prep(n):
        r0 = pl.multiple_of(n * c, c)
        rows = pl.ds(r0, c)
        q = qn[rows, :]
        k = kn[rows, :]
        v = vn[rows, :]
        beta = betab[rows, :]
        gc = cumsum_rows(gb[rows, :])
        diff = gc - gc.T
        decay = jnp.exp(jnp.where(incl, diff, NEG_INF))
        eg = jnp.exp(gc)
        g_last = gc[c - 1:c, :]
        kb = k * beta
        vb = v * beta
        kbf = k.astype(BF16)
        a2 = _dot_nt(jnp.concatenate([kb, q], axis=0).astype(BF16), kbf)
        lmat = jnp.where(strict, a2[:c] * decay, 0.0)
        intra = jnp.where(incl, a2[c:] * decay, 0.0)
        t = _tri_inverse(lmat, row, col)
        uw = _dot(t.astype(BF16), jnp.concatenate([kb * eg, vb], axis=1).astype(BF16))
        uwb = uw.astype(BF16)
        kdec = (k * jnp.exp(g_last - gc)).astype(BF16)
        kwu = _dot_tn(kdec, uwb)
        iwu = _dot(intra.astype(BF16), uwb)
        xs[n] = kwu[:, :HEAD_DIM].astype(BF16)
        cs[n] = kwu[:, HEAD_DIM:]
        qps[n] = (q * eg - iwu[:, :HEAD_DIM]).astype(BF16)
        olocs[n] = iwu[:, HEAD_DIM:]
        gls[n] = jnp.broadcast_to(jnp.exp(g_last), (SUBLANES, HEAD_DIM))

    def prep_body(i, carry):
        for u in range(GDN_UNROLL):
            chunk_prep(i * GDN_UNROLL + u)
        return carry

    lax.fori_loop(0, n_chunks // GDN_UNROLL, prep_body, 0)

    nw = nw_ref[...]
    state = jnp.zeros((HEAD_DIM, HEAD_DIM), F32)
    for n in range(n_chunks):
        rows = slice(n * c, (n + 1) * c)
        sb = state.astype(BF16)
        o = _dot(qps[n], sb) + olocs[n]
        o = o * lax.rsqrt(jnp.mean(o * o, axis=-1, keepdims=True) + NORM_EPS)
        o = o * nw * zs_ref[rows, :].astype(F32)
        o_ref[rows, :] = o.astype(BF16)
        state = state * gls[n][0:1, :] + cs[n] - _dot(xs[n], sb)


def _gdn(proj, conv_w, bg, norm_w, batch, seq):
    c = GDN_CHUNK
    n_chunks = seq // c
    blk = lambda off: pl.BlockSpec((seq, HEAD_DIM), lambda b, h: (b, off + h))
    cw = lambda off: pl.BlockSpec((CONV_WIDTH, HEAD_DIM), lambda b, h: (0, off + h))
    return pl.pallas_call(
        _gdn_kernel,
        grid=(batch, HEADS),
        in_specs=[
            blk(QKV_A), blk(QKV_A + HEADS), blk(QKV_A + 2 * HEADS),
            cw(0), cw(HEADS), cw(2 * HEADS),
            pl.BlockSpec((seq, LANES), lambda b, h: (b, 0)),
            blk(Z_A),
            pl.BlockSpec((1, HEAD_DIM), lambda b, h: (0, 0)),
        ],
        out_specs=pl.BlockSpec((seq, HEAD_DIM), lambda b, h: (b, h)),
        out_shape=jax.ShapeDtypeStruct((batch * seq, HEADS * HEAD_DIM), BF16),
        scratch_shapes=[
            pltpu.VMEM((seq + SUBLANES, HEAD_DIM), F32),
            pltpu.VMEM((seq, HEAD_DIM), F32),
            pltpu.VMEM((seq, HEAD_DIM), F32),
            pltpu.VMEM((seq, HEAD_DIM), F32),
            pltpu.VMEM((seq, LANES), F32),
            pltpu.VMEM((seq, LANES), F32),
            pltpu.VMEM((n_chunks, HEAD_DIM, HEAD_DIM), BF16),
            pltpu.VMEM((n_chunks, HEAD_DIM, HEAD_DIM), F32),
            pltpu.VMEM((n_chunks, c, HEAD_DIM), BF16),
            pltpu.VMEM((n_chunks, c, HEAD_DIM), F32),
            pltpu.VMEM((n_chunks, SUBLANES, HEAD_DIM), F32),
        ],
        compiler_params=pltpu.CompilerParams(
            dimension_semantics=("parallel", "parallel"), vmem_limit_bytes=VMEM_LIMIT),
        name="gdn",
    )(proj, proj, proj, conv_w, conv_w, conv_w, bg, proj, norm_w)


def _moba_kernel(q_ref, k_ref, v_ref, zs_ref, o_ref):
    s = q_ref.shape[0]
    blk = MOBA_BLOCK
    nb = s // blk
    scale = HEAD_DIM ** -0.5

    prow = lax.broadcasted_iota(jnp.int32, (LANES, s), 0)
    pcol = lax.broadcasted_iota(jnp.int32, (LANES, s), 1)
    pind = jnp.where(prow == pcol // blk, 1.0 / blk, 0.0).astype(BF16)
    kmean = _dot(pind, k_ref[...])
    km_hi = kmean.astype(BF16)
    km_lo = (kmean - km_hi.astype(F32)).astype(BF16)

    lane = lax.broadcasted_iota(jnp.int32, (blk, LANES), 1)
    qi = lax.broadcasted_iota(jnp.int32, (blk, blk), 0)
    ki = lax.broadcasted_iota(jnp.int32, (blk, blk), 1)
    causal = (ki <= qi).astype(F32)

    for i in range(nb):
        rows = slice(i * blk, (i + 1) * blk)
        q = q_ref[rows, :]
        nkeys = (i + 1) * blk
        sc = _dot_nt(q, k_ref[0:nkeys, :]) * scale
        masks = []
        if i > 0:
            if i > MOBA_TOPK:
                gate = _dot_nt(q, km_hi) + _dot_nt(q, km_lo)
                rank = jnp.zeros((blk, LANES), jnp.int32)
                for jp in range(i):
                    gj = jnp.broadcast_to(gate[:, jp:jp + 1], (blk, LANES))
                    ahead = jnp.logical_or(gj > gate, jnp.logical_and(gj == gate, lane > jp))
                    rank = rank + ahead.astype(jnp.int32)
                sel = (rank < MOBA_TOPK).astype(F32)
            else:
                sel = jnp.ones((blk, LANES), F32)
            for j in range(i):
                masks.append(jnp.broadcast_to(sel[:, j:j + 1], (blk, blk)))
        masks.append(causal)
        mask = jnp.concatenate(masks, axis=1) if len(masks) > 1 else masks[0]
        sc = jnp.where(mask > 0.5, sc, NEG_INF)
        m = jnp.max(sc, axis=-1, keepdims=True)
        p = jnp.exp(sc - m)
        l = jnp.sum(p, axis=-1, keepdims=True)
        o = _dot(p.astype(BF16), v_ref[0:nkeys, :]) / l
        o_ref[rows, :] = (o * zs_ref[rows, :].astype(F32)).astype(BF16)


def _moba(proj, batch, seq):
    blk = lambda off: pl.BlockSpec((seq, HEAD_DIM), lambda b, h: (b, off + h))
    return pl.pallas_call(
        _moba_kernel,
        grid=(batch, HEADS),
        in_specs=[blk(QKV_B), blk(QKV_B + HEADS), blk(QKV_B + 2 * HEADS), blk(Z_B)],
        out_specs=pl.BlockSpec((seq, HEAD_DIM), lambda b, h: (b, h)),
        out_shape=jax.ShapeDtypeStruct((batch * seq, HEADS * HEAD_DIM), BF16),
        compiler_params=pltpu.CompilerParams(
            dimension_semantics=("parallel", "parallel"), vmem_limit_bytes=VMEM_LIMIT),
        name="moba",
    )(proj, proj, proj, proj)


def _outproj_kernel(oa_ref, ob_ref, sa_ref, sb_ref, x_ref, wa_ref, wb_ref, wo_ref, pw_ref, o_ref):
    ua = _dot(oa_ref[...], wa_ref[...])
    ub = _dot(ob_ref[...], wb_ref[...])
    merged = sa_ref[...].astype(F32) * ua + sb_ref[...].astype(F32) * ub
    y = _dot(merged.astype(BF16), wo_ref[...])
    inv = lax.rsqrt(jnp.mean(y * y, axis=-1, keepdims=True) + NORM_EPS)
    o_ref[...] = x_ref[...] + y * inv * pw_ref[...]


def _outproj(oa, ob, proj, x2, wa, wb, wo, pw):
    m, d = x2.shape
    width = oa.shape[1]
    tm = OUT_TM
    const = lambda shape: pl.BlockSpec(shape, lambda i: (0, 0), pipeline_mode=pl.Buffered(1))
    return pl.pallas_call(
        _outproj_kernel,
        grid=(m // tm,),
        in_specs=[
            pl.BlockSpec((tm, width), lambda i: (i, 0)),
            pl.BlockSpec((tm, width), lambda i: (i, 0)),
            pl.BlockSpec((tm, d), lambda i: (i, GATE_A_BLK2048)),
            pl.BlockSpec((tm, d), lambda i: (i, GATE_B_BLK2048)),
            pl.BlockSpec((tm, d), lambda i: (i, 0)),
            const((width, d)), const((width, d)), const((d, d)), const((1, d)),
        ],
        out_specs=pl.BlockSpec((tm, d), lambda i: (i, 0)),
        out_shape=jax.ShapeDtypeStruct((m, d), F32),
        compiler_params=pltpu.CompilerParams(
            dimension_semantics=("parallel",), vmem_limit_bytes=VMEM_LIMIT),
        name="outproj",
    )(oa, ob, proj, proj, x2, wa, wb, wo, pw)


def _layer(x, pre_w, w_in, conv_w, a_log, dt_bias, gdn_norm_w, w_a, w_b, w_out, post_w):
    batch, seq, d = x.shape
    width = HEADS * HEAD_DIM
    ba_lo = 4 * width
    ba_hi = ba_lo + 2 * HEADS
    x2 = x.reshape(batch * seq, d)
    w_main = jnp.concatenate([w_in[:, :ba_lo], w_in[:, ba_hi:]], axis=1).astype(BF16)
    w_ba = jnp.pad(w_in[:, ba_lo:ba_hi], ((0, 0), (0, LANES - 2 * HEADS))).astype(BF16)
    pad_row = lambda v: jnp.pad(v.astype(F32), (HEADS, LANES - 2 * HEADS)).reshape(1, LANES)
    proj, bg = _inproj(x2, pre_w.reshape(1, d), w_main, w_ba, pad_row(a_log), pad_row(dt_bias))
    oa = _gdn(proj, conv_w, bg, gdn_norm_w.reshape(1, HEAD_DIM), batch, seq)
    ob = _moba(proj, batch, seq)
    out = _outproj(oa, ob, proj, x2, w_a.astype(BF16), w_b.astype(BF16), w_out.astype(BF16),
                   post_w.reshape(1, d))
    return out.reshape(batch, seq, d)


def kernel(x, pre_norm_w, w_in, conv_w, a_log, dt_bias, gdn_norm_w, w_branch_a, w_branch_b, w_out,
           post_norm_w):
    depth = pre_norm_w.shape[0]
    for l in range(depth):
        x = _layer(x, pre_norm_w[l], w_in[l], conv_w[l], a_log[l], dt_bias[l], gdn_norm_w[l],
                   w_branch_a[l], w_branch_b[l], w_out[l], post_norm_w[l])
    return x
```

```python
import functools

import jax
import jax.numpy as jnp
from jax import lax
from jax.experimental import pallas as pl
from jax.experimental.pallas import tpu as pltpu

F32 = jnp.float32
BF16 = jnp.bfloat16

NORM_EPS = 1e-6
NEG_INF = -1e30

HEADS = 8
HEAD_DIM = 128
CONV_WIDTH = 4
MOBA_BLOCK = 256
MOBA_TOPK = 3
GDN_CHUNK = 128
GDN_GROUP = 8

LANES = 128
SUBLANES = 8

QKV_A, Z_A, QKV_B, Z_B = 0, 24, 32, 56
GATE_A_BLK2048, GATE_B_BLK2048 = 4, 5

INPROJ_TM = 1024
INPROJ_TN = 1024
OUT_TM = 256

VMEM_LIMIT = 56 * 1024 * 1024


def _dot(a, b):
    return jnp.dot(a, b, preferred_element_type=F32)


def _dot_nt(a, b):
    return lax.dot_general(a, b, (((1,), (1,)), ((), ())), preferred_element_type=F32)


def _dot_tn(a, b):
    return lax.dot_general(a, b, (((0,), (0,)), ((), ())), preferred_element_type=F32)


def _sigmoid(x):
    return 1.0 / (1.0 + jnp.exp(-x))


def _softplus(x):
    return jnp.maximum(x, 0.0) + jnp.log1p(jnp.exp(-jnp.abs(x)))


def _inproj_kernel(x_ref, nw_ref, w_ref, wba_ref, alog_ref, dtb_ref, proj_ref, bg_ref, h_ref):
    j = pl.program_id(1)

    @pl.when(j == 0)
    def _():
        xf = x_ref[...]
        inv = lax.rsqrt(jnp.mean(xf * xf, axis=-1, keepdims=True) + NORM_EPS)
        h_ref[...] = (xf * inv * nw_ref[...]).astype(BF16)
        ba = _dot(h_ref[...], wba_ref[...])
        lane = lax.broadcasted_iota(jnp.int32, ba.shape, 1)
        beta = _sigmoid(ba)
        g = -jnp.exp(alog_ref[...]) * _softplus(ba + dtb_ref[...])
        bg_ref[...] = jnp.where(lane < HEADS, beta, g)

    acc = _dot(h_ref[...], w_ref[...])
    sig = _sigmoid(acc)
    is_silu = jnp.logical_or(j == 3, j == 7)
    is_sig = j >= 8
    out = jnp.where(is_sig, sig, acc * jnp.where(is_silu, sig, 1.0))
    proj_ref[...] = out.astype(BF16)


def _inproj(x2, nw, w_main, w_ba, alog_row, dtb_row):
    m, d = x2.shape
    n = w_main.shape[1]
    tm, tn = INPROJ_TM, INPROJ_TN
    return pl.pallas_call(
        _inproj_kernel,
        grid=(m // tm, n // tn),
        in_specs=[
            pl.BlockSpec((tm, d), lambda i, j: (i, 0)),
            pl.BlockSpec((1, d), lambda i, j: (0, 0)),
            pl.BlockSpec((d, tn), lambda i, j: (0, j)),
            pl.BlockSpec((d, LANES), lambda i, j: (0, 0)),
            pl.BlockSpec((1, LANES), lambda i, j: (0, 0)),
            pl.BlockSpec((1, LANES), lambda i, j: (0, 0)),
        ],
        out_specs=[
            pl.BlockSpec((tm, tn), lambda i, j: (i, j)),
            pl.BlockSpec((tm, LANES), lambda i, j: (i, 0)),
        ],
        out_shape=[
            jax.ShapeDtypeStruct((m, n), BF16),
            jax.ShapeDtypeStruct((m, LANES), F32),
        ],
        scratch_shapes=[pltpu.VMEM((tm, d), BF16)],
        compiler_params=pltpu.CompilerParams(
            dimension_semantics=("parallel", "arbitrary"), vmem_limit_bytes=VMEM_LIMIT),
        name="inproj",
    )(x2, nw, w_main, w_ba, alog_row, dtb_row)


def _tri_inverse_group(lmats, row, col):
    c = lmats[0].shape[0]

    def sub_mask(s_log):
        bi = jnp.right_shift(row, s_log)
        bj = jnp.right_shift(col, s_log)
        return jnp.logical_and(jnp.bitwise_xor(bi, bj) == 1, jnp.bitwise_and(bi, 1) == 1)

    eye = (row == col).astype(F32)
    m0 = sub_mask(0)
    tds = [eye - jnp.where(m0, lm, 0.0) for lm in lmats]
    s_log = 1
    while (1 << s_log) < c:
        ms = sub_mask(s_log)
        tbs = [td.astype(BF16) for td in tds]
        x1s = [_dot(tb, jnp.where(ms, lm, 0.0).astype(BF16)).astype(BF16)
               for tb, lm in zip(tbs, lmats)]
        tds = [td - _dot(x1, tb) for td, x1, tb in zip(tds, x1s, tbs)]
        s_log += 1
    return tds


def _gdn_kernel(q_ref, k_ref, v_ref, cwq_ref, cwk_ref, cwv_ref, bg_ref, zs_ref, nw_ref, o_ref,
                xpad, qn, kn, vn, betab, gb, xs, cs, qps, olocs, gls):
    s = q_ref.shape[0]
    c = GDN_CHUNK
    n_chunks = s // c
    h = pl.program_id(1)

    def conv_silu(src_ref, cw_ref):
        xpad[0:SUBLANES, :] = jnp.zeros((SUBLANES, HEAD_DIM), F32)
        xpad[SUBLANES:SUBLANES + s, :] = src_ref[...].astype(F32)
        base = SUBLANES - (CONV_WIDTH - 1)
        acc = cw_ref[0:1, :] * xpad[base:base + s, :]
        for j in range(1, CONV_WIDTH):
            acc = acc + cw_ref[j:j + 1, :] * xpad[base + j:base + j + s, :]
        return acc * _sigmoid(acc)

    def l2n(x):
        return x * lax.rsqrt(jnp.sum(x * x, axis=-1, keepdims=True) + NORM_EPS)

    qn[...] = l2n(conv_silu(q_ref, cwq_ref)) * (HEAD_DIM ** -0.5)
    kn[...] = l2n(conv_silu(k_ref, cwk_ref))
    vn[...] = conv_silu(v_ref, cwv_ref)

    bg = bg_ref[...]
    lane = lax.broadcasted_iota(jnp.int32, bg.shape, 1)
    beta_col = jnp.sum(jnp.where(lane == h, bg, 0.0), axis=-1, keepdims=True)
    g_col = jnp.sum(jnp.where(lane == h + HEADS, bg, 0.0), axis=-1, keepdims=True)
    betab[...] = jnp.broadcast_to(beta_col, bg.shape)
    gb[...] = jnp.broadcast_to(g_col, bg.shape)

    row = lax.broadcasted_iota(jnp.int32, (c, c), 0)
    col = lax.broadcasted_iota(jnp.int32, (c, c), 1)
    incl = row >= col
    strict = row > col
    tril_b = incl.astype(BF16)

    def cumsum_rows(x):
        x1 = x.astype(BF16)
        r1 = x - x1.astype(F32)
        x2 = r1.astype(BF16)
        x3 = (r1 - x2.astype(F32)).astype(BF16)
        return _dot(tril_b, x1) + _dot(tril_b, x2) + _dot(tril_b, x3)

    def prep_group(ns):
        rows = [slice(n * c, (n + 1) * c) for n in ns]
        gcs = [cumsum_rows(gb[r, :]) for r in rows]
        decays = [jnp.exp(jnp.where(incl, gc - gc.T, NEG_INF)) for gc in gcs]
        a2s = [_dot_nt(jnp.concatenate([kn[r, :] * betab[r, :], qn[r, :]], axis=0).astype(BF16),
                       kn[r, :].astype(BF16)) for r in rows]
        lmats = [jnp.where(strict, a2[:c] * d, 0.0) for a2, d in zip(a2s, decays)]
        ts = _tri_inverse_group(lmats, row, col)
        uwbs = []
        for r, gc, t in zip(rows, gcs, ts):
            beta = betab[r, :]
            rhs = jnp.concatenate([kn[r, :] * beta * jnp.exp(gc), vn[r, :] * beta], axis=1)
            uwbs.append(_dot(t.astype(BF16), rhs.astype(BF16)).astype(BF16))
        for n, r, gc, a2, d, uwb in zip(ns, rows, gcs, a2s, decays, uwbs):
            g_last = gc[c - 1:c, :]
            intra = jnp.where(incl, a2[c:] * d, 0.0)
            kdec = (kn[r, :] * jnp.exp(g_last - gc)).astype(BF16)
            kwu = _dot_tn(kdec, uwb)
            iwu = _dot(intra.astype(BF16), uwb)
            xs[n] = kwu[:, :HEAD_DIM].astype(BF16)
            cs[n] = kwu[:, HEAD_DIM:]
            qps[n] = (qn[r, :] * jnp.exp(gc) - iwu[:, :HEAD_DIM]).astype(BF16)
            olocs[n] = iwu[:, HEAD_DIM:]
            gls[n] = jnp.broadcast_to(jnp.exp(g_last), (SUBLANES, HEAD_DIM))

    for grp in range(n_chunks // GDN_GROUP):
        prep_group([grp * GDN_GROUP + u for u in range(GDN_GROUP)])

    nw = nw_ref[...]
    state = jnp.zeros((HEAD_DIM, HEAD_DIM), F32)
    for n in range(n_chunks):
        rows = slice(n * c, (n + 1) * c)
        sb = state.astype(BF16)
        o = _dot(qps[n], sb) + olocs[n]
        o = o * lax.rsqrt(jnp.mean(o * o, axis=-1, keepdims=True) + NORM_EPS)
        o = o * nw * zs_ref[rows, :].astype(F32)
        o_ref[rows, :] = o.astype(BF16)
        state = state * gls[n][0:1, :] + cs[n] - _dot(xs[n], sb)


def _gdn(proj, conv_w, bg, norm_w, batch, seq):
    c = GDN_CHUNK
    n_chunks = seq // c
    blk = lambda off: pl.BlockSpec((seq, HEAD_DIM), lambda b, h: (b, off + h))
    cw = lambda off: pl.BlockSpec((CONV_WIDTH, HEAD_DIM), lambda b, h: (0, off + h))
    return pl.pallas_call(
        _gdn_kernel,
        grid=(batch, HEADS),
        in_specs=[
            blk(QKV_A), blk(QKV_A + HEADS), blk(QKV_A + 2 * HEADS),
            cw(0), cw(HEADS), cw(2 * HEADS),
            pl.BlockSpec((seq, LANES), lambda b, h: (b, 0)),
            blk(Z_A),
            pl.BlockSpec((1, HEAD_DIM), lambda b, h: (0, 0)),
        ],
        out_specs=pl.BlockSpec((seq, HEAD_DIM), lambda b, h: (b, h)),
        out_shape=jax.ShapeDtypeStruct((batch * seq, HEADS * HEAD_DIM), BF16),
        scratch_shapes=[
            pltpu.VMEM((seq + SUBLANES, HEAD_DIM), F32),
            pltpu.VMEM((seq, HEAD_DIM), F32),
            pltpu.VMEM((seq, HEAD_DIM), F32),
            pltpu.VMEM((seq, HEAD_DIM), F32),
            pltpu.VMEM((seq, LANES), F32),
            pltpu.VMEM((seq, LANES), F32),
            pltpu.VMEM((n_chunks, HEAD_DIM, HEAD_DIM), BF16),
            pltpu.VMEM((n_chunks, HEAD_DIM, HEAD_DIM), F32),
            pltpu.VMEM((n_chunks, c, HEAD_DIM), BF16),
            pltpu.VMEM((n_chunks, c, HEAD_DIM), F32),
            pltpu.VMEM((n_chunks, SUBLANES, HEAD_DIM), F32),
        ],
        compiler_params=pltpu.CompilerParams(
            dimension_semantics=("parallel", "parallel"), vmem_limit_bytes=VMEM_LIMIT),
        name="gdn",
    )(proj, proj, proj, conv_w, conv_w, conv_w, bg, proj, norm_w)


def _moba_kernel(q_ref, k_ref, v_ref, zs_ref, o_ref):
    s = q_ref.shape[0]
    blk = MOBA_BLOCK
    nb = s // blk
    scale = HEAD_DIM ** -0.5

    prow = lax.broadcasted_iota(jnp.int32, (LANES, s), 0)
    pcol = lax.broadcasted_iota(jnp.int32, (LANES, s), 1)
    pind = jnp.where(prow == pcol // blk, 1.0 / blk, 0.0).astype(BF16)
    kmean = _dot(pind, k_ref[...])
    km_hi = kmean.astype(BF16)
    km_lo = (kmean - km_hi.astype(F32)).astype(BF16)

    lane = lax.broadcasted_iota(jnp.int32, (blk, LANES), 1)
    qi = lax.broadcasted_iota(jnp.int32, (blk, blk), 0)
    ki = lax.broadcasted_iota(jnp.int32, (blk, blk), 1)
    causal = (ki <= qi).astype(F32)

    for i in range(nb):
        rows = slice(i * blk, (i + 1) * blk)
        q = q_ref[rows, :]
        nkeys = (i + 1) * blk
        sc = _dot_nt(q, k_ref[0:nkeys, :]) * scale
        masks = []
        if i > 0:
            if i > MOBA_TOPK:
                gate = _dot_nt(q, km_hi) + _dot_nt(q, km_lo)
                rank = jnp.zeros((blk, LANES), jnp.int32)
                for jp in range(i):
                    gj = jnp.broadcast_to(gate[:, jp:jp + 1], (blk, LANES))
                    ahead = jnp.logical_or(gj > gate, jnp.logical_and(gj == gate, lane > jp))
                    rank = rank + ahead.astype(jnp.int32)
                sel = (rank < MOBA_TOPK).astype(F32)
            else:
                sel = jnp.ones((blk, LANES), F32)
            for j in range(i):
                masks.append(jnp.broadcast_to(sel[:, j:j + 1], (blk, blk)))
        masks.append(causal)
        mask = jnp.concatenate(masks, axis=1) if len(masks) > 1 else masks[0]
        sc = jnp.where(mask > 0.5, sc, NEG_INF)
        m = jnp.max(sc, axis=-1, keepdims=True)
        p = jnp.exp(sc - m)
        l = jnp.sum(p, axis=-1, keepdims=True)
        o = _dot(p.astype(BF16), v_ref[0:nkeys, :]) / l
        o_ref[rows, :] = (o * zs_ref[rows, :].astype(F32)).astype(BF16)


def _moba(proj, batch, seq):
    blk = lambda off: pl.BlockSpec((seq, HEAD_DIM), lambda b, h: (b, off + h))
    return pl.pallas_call(
        _moba_kernel,
        grid=(batch, HEADS),
        in_specs=[blk(QKV_B), blk(QKV_B + HEADS), blk(QKV_B + 2 * HEADS), blk(Z_B)],
        out_specs=pl.BlockSpec((seq, HEAD_DIM), lambda b, h: (b, h)),
        out_shape=jax.ShapeDtypeStruct((batch * seq, HEADS * HEAD_DIM), BF16),
        compiler_params=pltpu.CompilerParams(
            dimension_semantics=("parallel", "parallel"), vmem_limit_bytes=VMEM_LIMIT),
        name="moba",
    )(proj, proj, proj, proj)


def _outproj_kernel(oa_ref, ob_ref, sa_ref, sb_ref, x_ref, wa_ref, wb_ref, wo_ref, pw_ref, o_ref):
    ua = _dot(oa_ref[...], wa_ref[...])
    ub = _dot(ob_ref[...], wb_ref[...])
    merged = sa_ref[...].astype(F32) * ua + sb_ref[...].astype(F32) * ub
    y = _dot(merged.astype(BF16), wo_ref[...])
    inv = lax.rsqrt(jnp.mean(y * y, axis=-1, keepdims=True) + NORM_EPS)
    o_ref[...] = x_ref[...] + y * inv * pw_ref[...]


def _outproj(oa, ob, proj, x2, wa, wb, wo, pw):
    m, d = x2.shape
    width = oa.shape[1]
    tm = OUT_TM
    const = lambda shape: pl.BlockSpec(shape, lambda i: (0, 0), pipeline_mode=pl.Buffered(1))
    return pl.pallas_call(
        _outproj_kernel,
        grid=(m // tm,),
        in_specs=[
            pl.BlockSpec((tm, width), lambda i: (i, 0)),
            pl.BlockSpec((tm, width), lambda i: (i, 0)),
            pl.BlockSpec((tm, d), lambda i: (i, GATE_A_BLK2048)),
            pl.BlockSpec((tm, d), lambda i: (i, GATE_B_BLK2048)),
            pl.BlockSpec((tm, d), lambda i: (i, 0)),
            const((width, d)), const((width, d)), const((d, d)), const((1, d)),
        ],
        out_specs=pl.BlockSpec((tm, d), lambda i: (i, 0)),
        out_shape=jax.ShapeDtypeStruct((m, d), F32),
        compiler_params=pltpu.CompilerParams(
            dimension_semantics=("parallel",), vmem_limit_bytes=VMEM_LIMIT),
        name="outproj",
    )(oa, ob, proj, proj, x2, wa, wb, wo, pw)


def _layer(x, pre_w, w_in, conv_w, a_log, dt_bias, gdn_norm_w, w_a, w_b, w_out, post_w):
    batch, seq, d = x.shape
    width = HEADS * HEAD_DIM
    ba_lo = 4 * width
    ba_hi = ba_lo + 2 * HEADS
    x2 = x.reshape(batch * seq, d)
    w_main = jnp.concatenate([w_in[:, :ba_lo], w_in[:, ba_hi:]], axis=1).astype(BF16)
    w_ba = jnp.pad(w_in[:, ba_lo:ba_hi], ((0, 0), (0, LANES - 2 * HEADS))).astype(BF16)
    pad_row = lambda v: jnp.pad(v.astype(F32), (HEADS, LANES - 2 * HEADS)).reshape(1, LANES)
    proj, bg = _inproj(x2, pre_w.reshape(1, d), w_main, w_ba, pad_row(a_log), pad_row(dt_bias))
    oa = _gdn(proj, conv_w, bg, gdn_norm_w.reshape(1, HEAD_DIM), batch, seq)
    ob = _moba(proj, batch, seq)
    out = _outproj(oa, ob, proj, x2, w_a.astype(BF16), w_b.astype(BF16), w_out.astype(BF16),
                   post_w.reshape(1, d))
    return out.reshape(batch, seq, d)


def kernel(x, pre_norm_w, w_in, conv_w, a_log, dt_bias, gdn_norm_w, w_branch_a, w_branch_b, w_out,
           post_norm_w):
    depth = pre_norm_w.shape[0]
    for l in range(depth):
        x = _layer(x, pre_norm_w[l], w_in[l], conv_w[l], a_log[l], dt_bias[l], gdn_norm_w[l],
                   w_branch_a[l], w_branch_b[l], w_out[l], post_norm_w[l])
    return x
```

```python
import functools

import jax
import jax.numpy as jnp
from jax import lax
from jax.experimental import pallas as pl
from jax.experimental.pallas import tpu as pltpu

F32 = jnp.float32
BF16 = jnp.bfloat16

NORM_EPS = 1e-6
NEG_INF = -1e30

HEADS = 8
HEAD_DIM = 128
CONV_WIDTH = 4
MOBA_BLOCK = 256
MOBA_TOPK = 3
GDN_CHUNK = 128
GDN_GROUP = 16

LANES = 128
SUBLANES = 8

QKV_A, Z_A, QKV_B, Z_B = 0, 24, 32, 56
GATE_A_BLK2048, GATE_B_BLK2048 = 4, 5

INPROJ_TM = 1024
INPROJ_TN = 1024
TILE_BLKS = INPROJ_TN // LANES
N_LO_TILES = (4 * HEADS * HEAD_DIM) // INPROJ_TN
OUT_TM = 256

MOBA_QSCALE = (HEAD_DIM ** -0.5) * 1.4426950408889634

VMEM_LIMIT = 56 * 1024 * 1024


def _dot(a, b):
    return jnp.dot(a, b, preferred_element_type=F32)


def _dot_nt(a, b):
    return lax.dot_general(a, b, (((1,), (1,)), ((), ())), preferred_element_type=F32)


def _dot_tn(a, b):
    return lax.dot_general(a, b, (((0,), (0,)), ((), ())), preferred_element_type=F32)


def _sigmoid(x):
    return 0.5 * jnp.tanh(0.5 * x) + 0.5


def _softplus(x):
    return jnp.maximum(x, 0.0) + jnp.log1p(jnp.exp(-jnp.abs(x)))


def _inproj_kernel(x_ref, nw_ref, wlo_ref, whi_ref, wba_ref, alog_ref, dtb_ref, proj_ref, bg_ref,
                   h_ref):
    j = pl.program_id(1)

    @pl.when(j == 0)
    def _():
        xf = x_ref[...]
        inv = lax.rsqrt(jnp.mean(xf * xf, axis=-1, keepdims=True) + NORM_EPS)
        h_ref[...] = (xf * inv * nw_ref[...]).astype(BF16)
        ba = _dot(h_ref[...], wba_ref[...])
        lane = lax.broadcasted_iota(jnp.int32, ba.shape, 1)
        beta = _sigmoid(ba)
        g = -jnp.exp(alog_ref[...]) * _softplus(ba + dtb_ref[...])
        bg_ref[...] = jnp.where(lane < HEADS, beta, g)

    is_silu = jnp.logical_or(j == Z_A // TILE_BLKS, j == Z_B // TILE_BLKS)
    is_sig = j >= GATE_A_BLK2048 * 2
    lin = jnp.where(j == QKV_B // TILE_BLKS, MOBA_QSCALE, 1.0)

    def tile(w_ref):
        acc = _dot(h_ref[...], w_ref[...])

        @pl.when(jnp.logical_not(jnp.logical_or(is_silu, is_sig)))
        def _():
            proj_ref[...] = (acc * lin).astype(BF16)

        @pl.when(is_silu)
        def _():
            proj_ref[...] = (acc * _sigmoid(acc)).astype(BF16)

        @pl.when(is_sig)
        def _():
            proj_ref[...] = _sigmoid(acc).astype(BF16)

    @pl.when(j < N_LO_TILES)
    def _():
        tile(wlo_ref)

    @pl.when(j >= N_LO_TILES)
    def _():
        tile(whi_ref)


def _inproj(x2, nw, w_lo, w_hi, w_ba, alog_row, dtb_row):
    m, d = x2.shape
    n = w_lo.shape[1] + w_hi.shape[1]
    tm, tn = INPROJ_TM, INPROJ_TN
    n_lo = w_lo.shape[1] // tn
    assert n_lo == N_LO_TILES
    return pl.pallas_call(
        _inproj_kernel,
        grid=(m // tm, n // tn),
        in_specs=[
            pl.BlockSpec((tm, d), lambda i, j: (i, 0)),
            pl.BlockSpec((1, d), lambda i, j: (0, 0)),
            pl.BlockSpec((d, tn), lambda i, j: (0, jnp.minimum(j, n_lo - 1))),
            pl.BlockSpec((d, tn), lambda i, j: (0, jnp.maximum(j - n_lo, 0))),
            pl.BlockSpec((d, LANES), lambda i, j: (0, 0)),
            pl.BlockSpec((1, LANES), lambda i, j: (0, 0)),
            pl.BlockSpec((1, LANES), lambda i, j: (0, 0)),
        ],
        out_specs=[
            pl.BlockSpec((tm, tn), lambda i, j: (i, j)),
            pl.BlockSpec((tm, LANES), lambda i, j: (i, 0)),
        ],
        out_shape=[
            jax.ShapeDtypeStruct((m, n), BF16),
            jax.ShapeDtypeStruct((m, LANES), F32),
        ],
        scratch_shapes=[pltpu.VMEM((tm, d), BF16)],
        compiler_params=pltpu.CompilerParams(
            dimension_semantics=("parallel", "arbitrary"), vmem_limit_bytes=VMEM_LIMIT),
        name="inproj",
    )(x2, nw, w_lo, w_hi, w_ba, alog_row, dtb_row)


def _tri_inverse_group(lmats, row, col):
    c = lmats[0].shape[0]

    def sub_mask(s_log):
        bi = jnp.right_shift(row, s_log)
        bj = jnp.right_shift(col, s_log)
        return jnp.logical_and(jnp.bitwise_xor(bi, bj) == 1, jnp.bitwise_and(bi, 1) == 1)

    eye = (row == col).astype(F32)
    m0 = sub_mask(0)
    tds = [eye - jnp.where(m0, lm, 0.0) for lm in lmats]
    s_log = 1
    while (1 << s_log) < c:
        ms = sub_mask(s_log)
        tbs = [td.astype(BF16) for td in tds]
        x1s = [_dot(tb, jnp.where(ms, lm, 0.0).astype(BF16)).astype(BF16)
               for tb, lm in zip(tbs, lmats)]
        tds = [td - _dot(x1, tb) for td, x1, tb in zip(tds, x1s, tbs)]
        s_log += 1
    return tds


def _gdn_kernel(q_ref, k_ref, v_ref, cwq_ref, cwk_ref, cwv_ref, bg_ref, zs_ref, nw_ref, o_ref,
                xpad, qn, kn, vn, betab, gb, xs, cs, qps, olocs, gls):
    s = q_ref.shape[0]
    c = GDN_CHUNK
    n_chunks = s // c
    h = pl.program_id(1)

    def conv_silu(src_ref, cw_ref):
        xpad[0:SUBLANES, :] = jnp.zeros((SUBLANES, HEAD_DIM), F32)
        xpad[SUBLANES:SUBLANES + s, :] = src_ref[...].astype(F32)
        base = SUBLANES - (CONV_WIDTH - 1)
        acc = cw_ref[0:1, :] * xpad[base:base + s, :]
        for j in range(1, CONV_WIDTH):
            acc = acc + cw_ref[j:j + 1, :] * xpad[base + j:base + j + s, :]
        return acc * _sigmoid(acc)

    def l2n(x):
        return x * lax.rsqrt(jnp.sum(x * x, axis=-1, keepdims=True) + NORM_EPS)

    qn[...] = l2n(conv_silu(q_ref, cwq_ref)) * (HEAD_DIM ** -0.5)
    kn[...] = l2n(conv_silu(k_ref, cwk_ref))
    vn[...] = conv_silu(v_ref, cwv_ref)

    bg = bg_ref[...]
    lane = lax.broadcasted_iota(jnp.int32, bg.shape, 1)
    beta_col = jnp.sum(jnp.where(lane == h, bg, 0.0), axis=-1, keepdims=True)
    g_col = jnp.sum(jnp.where(lane == h + HEADS, bg, 0.0), axis=-1, keepdims=True)
    betab[...] = jnp.broadcast_to(beta_col, bg.shape)
    gb[...] = jnp.broadcast_to(g_col, bg.shape)

    row = lax.broadcasted_iota(jnp.int32, (c, c), 0)
    col = lax.broadcasted_iota(jnp.int32, (c, c), 1)
    incl = row >= col
    strict = row > col
    tril_b = incl.astype(BF16)

    def cumsum_rows(x):
        x1 = x.astype(BF16)
        r1 = x - x1.astype(F32)
        x2 = r1.astype(BF16)
        x3 = (r1 - x2.astype(F32)).astype(BF16)
        return _dot(tril_b, x1) + _dot(tril_b, x2) + _dot(tril_b, x3)

    def prep_group(ns):
        rows = [slice(n * c, (n + 1) * c) for n in ns]
        gcs = [cumsum_rows(gb[r, :]) for r in rows]
        decays = [jnp.exp(jnp.where(incl, gc - gc.T, NEG_INF)) for gc in gcs]
        a2s = [_dot_nt(jnp.concatenate([kn[r, :] * betab[r, :], qn[r, :]], axis=0).astype(BF16),
                       kn[r, :].astype(BF16)) for r in rows]
        lmats = [jnp.where(strict, a2[:c] * d, 0.0) for a2, d in zip(a2s, decays)]
        ts = _tri_inverse_group(lmats, row, col)
        uwbs = []
        for r, gc, t in zip(rows, gcs, ts):
            beta = betab[r, :]
            rhs = jnp.concatenate([kn[r, :] * beta * jnp.exp(gc), vn[r, :] * beta], axis=1)
            uwbs.append(_dot(t.astype(BF16), rhs.astype(BF16)).astype(BF16))
        for n, r, gc, a2, d, uwb in zip(ns, rows, gcs, a2s, decays, uwbs):
            g_last = gc[c - 1:c, :]
            intra = jnp.where(incl, a2[c:] * d, 0.0)
            kdec = (kn[r, :] * jnp.exp(g_last - gc)).astype(BF16)
            kwu = _dot_tn(kdec, uwb)
            iwu = _dot(intra.astype(BF16), uwb)
            xs[n] = kwu[:, :HEAD_DIM].astype(BF16)
            cs[n] = kwu[:, HEAD_DIM:]
            qps[n] = (qn[r, :] * jnp.exp(gc) - iwu[:, :HEAD_DIM]).astype(BF16)
            olocs[n] = iwu[:, HEAD_DIM:]
            gls[n] = jnp.broadcast_to(jnp.exp(g_last), (SUBLANES, HEAD_DIM))

    for grp in range(n_chunks // GDN_GROUP):
        prep_group([grp * GDN_GROUP + u for u in range(GDN_GROUP)])

    nw = nw_ref[...]
    state = jnp.zeros((HEAD_DIM, HEAD_DIM), F32)
    for n in range(n_chunks):
        rows = slice(n * c, (n + 1) * c)
        sb = state.astype(BF16)
        o = _dot(qps[n], sb) + olocs[n]
        o = o * lax.rsqrt(jnp.mean(o * o, axis=-1, keepdims=True) + NORM_EPS)
        o = o * nw * zs_ref[rows, :].astype(F32)
        o_ref[rows, :] = o.astype(BF16)
        state = state * gls[n][0:1, :] + cs[n] - _dot(xs[n], sb)


def _gdn(proj, conv_w, bg, norm_w, batch, seq):
    c = GDN_CHUNK
    n_chunks = seq // c
    blk = lambda off: pl.BlockSpec((seq, HEAD_DIM), lambda b, h: (b, off + h))
    cw = lambda off: pl.BlockSpec((CONV_WIDTH, HEAD_DIM), lambda b, h: (0, off + h))
    return pl.pallas_call(
        _gdn_kernel,
        grid=(batch, HEADS),
        in_specs=[
            blk(QKV_A), blk(QKV_A + HEADS), blk(QKV_A + 2 * HEADS),
            cw(0), cw(HEADS), cw(2 * HEADS),
            pl.BlockSpec((seq, LANES), lambda b, h: (b, 0)),
            blk(Z_A),
            pl.BlockSpec((1, HEAD_DIM), lambda b, h: (0, 0)),
        ],
        out_specs=pl.BlockSpec((seq, HEAD_DIM), lambda b, h: (b, h)),
        out_shape=jax.ShapeDtypeStruct((batch * seq, HEADS * HEAD_DIM), BF16),
        scratch_shapes=[
            pltpu.VMEM((seq + SUBLANES, HEAD_DIM), F32),
            pltpu.VMEM((seq, HEAD_DIM), F32),
            pltpu.VMEM((seq, HEAD_DIM), F32),
            pltpu.VMEM((seq, HEAD_DIM), F32),
            pltpu.VMEM((seq, LANES), F32),
            pltpu.VMEM((seq, LANES), F32),
            pltpu.VMEM((n_chunks, HEAD_DIM, HEAD_DIM), BF16),
            pltpu.VMEM((n_chunks, HEAD_DIM, HEAD_DIM), F32),
            pltpu.VMEM((n_chunks, c, HEAD_DIM), BF16),
            pltpu.VMEM((n_chunks, c, HEAD_DIM), F32),
            pltpu.VMEM((n_chunks, SUBLANES, HEAD_DIM), F32),
        ],
        compiler_params=pltpu.CompilerParams(
            dimension_semantics=("parallel", "parallel"), vmem_limit_bytes=VMEM_LIMIT),
        name="gdn",
    )(proj, proj, proj, conv_w, conv_w, conv_w, bg, proj, norm_w)


def _moba_kernel(q_ref, k_ref, v_ref, zs_ref, o_ref):
    s = q_ref.shape[0]
    blk = MOBA_BLOCK
    nb = s // blk

    prow = lax.broadcasted_iota(jnp.int32, (LANES, s), 0)
    pcol = lax.broadcasted_iota(jnp.int32, (LANES, s), 1)
    pind = jnp.where(prow == pcol // blk, 1.0 / blk, 0.0).astype(BF16)
    kmean = _dot(pind, k_ref[...])
    cand = 2 * SUBLANES
    km_full = kmean.astype(BF16)
    km_hi = km_full[0:cand, :]
    km_lo = (kmean - km_full.astype(F32)).astype(BF16)[0:cand, :]

    sub = lax.broadcasted_iota(jnp.int32, (cand, blk), 0)
    r16 = lax.broadcasted_iota(jnp.int32, (cand, LANES), 0)
    l16 = lax.broadcasted_iota(jnp.int32, (cand, LANES), 1)
    neg_eye = jnp.where(r16 == l16, NEG_INF, 0.0).astype(BF16)
    qi = lax.broadcasted_iota(jnp.int32, (blk, blk), 0)
    ki = lax.broadcasted_iota(jnp.int32, (blk, blk), 1)
    causal = ki <= qi

    def score_stage(i):
        q = q_ref[i * blk:(i + 1) * blk, :]
        pieces = [_dot_nt(q, k_ref[j * blk:(j + 1) * blk, :]) for j in range(i + 1)]
        gt = _dot_nt(km_hi, q) + _dot_nt(km_lo, q) if i > MOBA_TOPK else None
        return pieces, gt

    staged = score_stage(0)
    for i in range(nb):
        rows = slice(i * blk, (i + 1) * blk)
        pieces, gt = staged
        if i + 1 < nb:
            staged = score_stage(i + 1)
        if i > MOBA_TOPK:
            rank = jnp.zeros((cand, blk), jnp.int32)
            for jp in range(i):
                gj = jnp.broadcast_to(gt[jp:jp + 1, :], (cand, blk))
                ahead = jnp.logical_or(gj > gt, jnp.logical_and(gj == gt, sub > jp))
                rank = rank + ahead.astype(jnp.int32)
            drop = (rank >= MOBA_TOPK).astype(BF16)
            bias = _dot_tn(drop, neg_eye)
            for j in range(i):
                pieces[j] = pieces[j] + jnp.broadcast_to(bias[:, j:j + 1], (blk, blk))
        pieces[i] = jnp.where(causal, pieces[i], NEG_INF)
        mt = pieces[0]
        for pc in pieces[1:]:
            mt = jnp.maximum(mt, pc)
        m = jnp.max(mt, axis=-1, keepdims=True)
        pt = None
        o = jnp.zeros((blk, HEAD_DIM), F32)
        for j, pc in enumerate(pieces):
            p = jnp.exp2(pc - m)
            pt = p if pt is None else pt + p
            o = o + _dot(p.astype(BF16), v_ref[j * blk:(j + 1) * blk, :])
        l = jnp.sum(pt, axis=-1, keepdims=True)
        o_ref[rows, :] = (o / l * zs_ref[rows, :].astype(F32)).astype(BF16)


def _moba(proj, batch, seq):
    blk = lambda off: pl.BlockSpec((seq, HEAD_DIM), lambda b, h: (b, off + h))
    return pl.pallas_call(
        _moba_kernel,
        grid=(batch, HEADS),
        in_specs=[blk(QKV_B), blk(QKV_B + HEADS), blk(QKV_B + 2 * HEADS), blk(Z_B)],
        out_specs=pl.BlockSpec((seq, HEAD_DIM), lambda b, h: (b, h)),
        out_shape=jax.ShapeDtypeStruct((batch * seq, HEADS * HEAD_DIM), BF16),
        compiler_params=pltpu.CompilerParams(
            dimension_semantics=("parallel", "parallel"), vmem_limit_bytes=VMEM_LIMIT),
        name="moba",
    )(proj, proj, proj, proj)


def _outproj_kernel(oa_ref, ob_ref, sa_ref, sb_ref, x_ref, wa_ref, wb_ref, wo_ref, pw_ref, o_ref):
    ua = _dot(oa_ref[...], wa_ref[...])
    ub = _dot(ob_ref[...], wb_ref[...])
    merged = sa_ref[...].astype(F32) * ua + sb_ref[...].astype(F32) * ub
    y = _dot(merged.astype(BF16), wo_ref[...])
    inv = lax.rsqrt(jnp.mean(y * y, axis=-1, keepdims=True) + NORM_EPS)
    o_ref[...] = x_ref[...] + y * inv * pw_ref[...]


def _outproj(oa, ob, proj, x2, wa, wb, wo, pw):
    m, d = x2.shape
    width = oa.shape[1]
    tm = OUT_TM
    const = lambda shape: pl.BlockSpec(shape, lambda i: (0, 0), pipeline_mode=pl.Buffered(1))
    return pl.pallas_call(
        _outproj_kernel,
        grid=(m // tm,),
        in_specs=[
            pl.BlockSpec((tm, width), lambda i: (i, 0)),
            pl.BlockSpec((tm, width), lambda i: (i, 0)),
            pl.BlockSpec((tm, d), lambda i: (i, GATE_A_BLK2048)),
            pl.BlockSpec((tm, d), lambda i: (i, GATE_B_BLK2048)),
            pl.BlockSpec((tm, d), lambda i: (i, 0)),
            const((width, d)), const((width, d)), const((d, d)), const((1, d)),
        ],
        out_specs=pl.BlockSpec((tm, d), lambda i: (i, 0)),
        out_shape=jax.ShapeDtypeStruct((m, d), F32),
        compiler_params=pltpu.CompilerParams(
            dimension_semantics=("parallel",), vmem_limit_bytes=VMEM_LIMIT),
        name="outproj",
    )(oa, ob, proj, proj, x2, wa, wb, wo, pw)


def _layer(x, pre_w, w_in, conv_w, a_log, dt_bias, gdn_norm_w, w_a, w_b, w_out, post_w):
    batch, seq, d = x.shape
    width = HEADS * HEAD_DIM
    ba_lo = 4 * width
    ba_hi = ba_lo + 2 * HEADS
    x2 = x.reshape(batch * seq, d)
    w_lo = w_in[:, :ba_lo].astype(BF16)
    w_hi = w_in[:, ba_hi:].astype(BF16)
    w_ba = jnp.pad(w_in[:, ba_lo:ba_hi], ((0, 0), (0, LANES - 2 * HEADS))).astype(BF16)
    pad_row = lambda v: jnp.pad(v.astype(F32), (HEADS, LANES - 2 * HEADS)).reshape(1, LANES)
    proj, bg = _inproj(x2, pre_w.reshape(1, d), w_lo, w_hi, w_ba, pad_row(a_log), pad_row(dt_bias))
    oa = _gdn(proj, conv_w, bg, gdn_norm_w.reshape(1, HEAD_DIM), batch, seq)
    ob = _moba(proj, batch, seq)
    out = _outproj(oa, ob, proj, x2, w_a.astype(BF16), w_b.astype(BF16), w_out.astype(BF16),
                   post_w.reshape(1, d))
    return out.reshape(batch, seq, d)


def kernel(x, pre_norm_w, w_in, conv_w, a_log, dt_bias, gdn_norm_w, w_branch_a, w_branch_b, w_out,
           post_norm_w):
    depth = pre_norm_w.shape[0]
    for l in range(depth):
        x = _layer(x, pre_norm_w[l], w_in[l], conv_w[l], a_log[l], dt_bias[l], gdn_norm_w[l],
                   w_branch_a[l], w_branch_b[l], w_out[l], post_norm_w[l])
    return x
```

```python
import functools

import jax
import jax.numpy as jnp
from jax import lax
from jax.experimental import pallas as pl
from jax.experimental.pallas import tpu as pltpu

F32 = jnp.float32
BF16 = jnp.bfloat16

NORM_EPS = 1e-6
NEG_INF = -1e30

HEADS = 8
HEAD_DIM = 128
CONV_WIDTH = 4
MOBA_BLOCK = 256
MOBA_TOPK = 3
GDN_CHUNK = 128
GDN_GROUP = 16

LANES = 128
SUBLANES = 8

QKV_A, Z_A, QKV_B, Z_B = 0, 24, 32, 56
GATE_A_BLK2048, GATE_B_BLK2048 = 4, 5

INPROJ_TM = 1024
INPROJ_TN = 1024
TILE_BLKS = INPROJ_TN // LANES
N_LO_TILES = (4 * HEADS * HEAD_DIM) // INPROJ_TN
OUT_TM = 256

MOBA_QSCALE = (HEAD_DIM ** -0.5) * 1.4426950408889634

VMEM_LIMIT = 56 * 1024 * 1024


def _dot(a, b):
    return jnp.dot(a, b, preferred_element_type=F32)


def _dot_nt(a, b):
    return lax.dot_general(a, b, (((1,), (1,)), ((), ())), preferred_element_type=F32)


def _dot_tn(a, b):
    return lax.dot_general(a, b, (((0,), (0,)), ((), ())), preferred_element_type=F32)


def _sigmoid(x):
    return 0.5 * jnp.tanh(0.5 * x) + 0.5


def _softplus(x):
    return jnp.maximum(x, 0.0) + jnp.log1p(jnp.exp(-jnp.abs(x)))


def _inproj_kernel(x_ref, nw_ref, wlo_ref, whi_ref, wba_ref, alog_ref, dtb_ref, proj_ref, bg_ref,
                   h_ref):
    j = pl.program_id(1)

    @pl.when(j == 0)
    def _():
        xf = x_ref[...]
        inv = lax.rsqrt(jnp.mean(xf * xf, axis=-1, keepdims=True) + NORM_EPS)
        h_ref[...] = (xf * inv * nw_ref[...]).astype(BF16)
        ba = _dot(h_ref[...], wba_ref[...])
        lane = lax.broadcasted_iota(jnp.int32, ba.shape, 1)
        beta = _sigmoid(ba)
        g = -jnp.exp(alog_ref[...]) * _softplus(ba + dtb_ref[...])
        bg_ref[...] = jnp.where(lane < HEADS, beta, g)

    is_silu = jnp.logical_or(j == Z_A // TILE_BLKS, j == Z_B // TILE_BLKS)
    is_sig = j >= GATE_A_BLK2048 * 2
    lin = jnp.where(j == QKV_B // TILE_BLKS, MOBA_QSCALE, 1.0)

    def tile(w_ref):
        acc = _dot(h_ref[...], w_ref[...])

        @pl.when(jnp.logical_not(jnp.logical_or(is_silu, is_sig)))
        def _():
            proj_ref[...] = (acc * lin).astype(BF16)

        @pl.when(is_silu)
        def _():
            proj_ref[...] = (acc * _sigmoid(acc)).astype(BF16)

        @pl.when(is_sig)
        def _():
            proj_ref[...] = _sigmoid(acc).astype(BF16)

    @pl.when(j < N_LO_TILES)
    def _():
        tile(wlo_ref)

    @pl.when(j >= N_LO_TILES)
    def _():
        tile(whi_ref)


def _inproj(x2, nw, w_lo, w_hi, w_ba, alog_row, dtb_row):
    m, d = x2.shape
    tm, tn = INPROJ_TM, INPROJ_TN
    n_lo = N_LO_TILES
    assert w_lo.shape[1] >= n_lo * tn and w_hi.shape[1] % tn == 0
    n = n_lo * tn + w_hi.shape[1]
    return pl.pallas_call(
        _inproj_kernel,
        grid=(m // tm, n // tn),
        in_specs=[
            pl.BlockSpec((tm, d), lambda i, j: (i, 0)),
            pl.BlockSpec((1, d), lambda i, j: (0, 0)),
            pl.BlockSpec((d, tn), lambda i, j: (0, jnp.minimum(j, n_lo - 1))),
            pl.BlockSpec((d, tn), lambda i, j: (0, jnp.maximum(j - n_lo, 0))),
            pl.BlockSpec((d, LANES), lambda i, j: (0, 0)),
            pl.BlockSpec((1, LANES), lambda i, j: (0, 0)),
            pl.BlockSpec((1, LANES), lambda i, j: (0, 0)),
        ],
        out_specs=[
            pl.BlockSpec((tm, tn), lambda i, j: (i, j)),
            pl.BlockSpec((tm, LANES), lambda i, j: (i, 0)),
        ],
        out_shape=[
            jax.ShapeDtypeStruct((m, n), BF16),
            jax.ShapeDtypeStruct((m, LANES), F32),
        ],
        scratch_shapes=[pltpu.VMEM((tm, d), BF16)],
        compiler_params=pltpu.CompilerParams(
            dimension_semantics=("parallel", "arbitrary"), vmem_limit_bytes=VMEM_LIMIT),
        name="inproj",
    )(x2, nw, w_lo, w_hi, w_ba, alog_row, dtb_row)


def _tri_inverse_group(lmats, row, col):
    c = lmats[0].shape[0]

    def sub_mask(s_log):
        bi = jnp.right_shift(row, s_log)
        bj = jnp.right_shift(col, s_log)
        return jnp.logical_and(jnp.bitwise_xor(bi, bj) == 1, jnp.bitwise_and(bi, 1) == 1)

    eye = (row == col).astype(F32)
    m0 = sub_mask(0)
    tds = [eye - jnp.where(m0, lm, 0.0) for lm in lmats]
    s_log = 1
    while (1 << s_log) < c:
        ms = sub_mask(s_log)
        tbs = [td.astype(BF16) for td in tds]
        x1s = [_dot(tb, jnp.where(ms, lm, 0.0).astype(BF16)).astype(BF16)
               for tb, lm in zip(tbs, lmats)]
        tds = [td - _dot(x1, tb) for td, x1, tb in zip(tds, x1s, tbs)]
        s_log += 1
    return tds


def _gdn_kernel(q_ref, k_ref, v_ref, cwq_ref, cwk_ref, cwv_ref, bg_ref, zs_ref, nw_ref, o_ref,
                xpad, qn, kn, vn, betab, gb, xs, cs, qps, olocs, gls):
    s = q_ref.shape[0]
    c = GDN_CHUNK
    n_chunks = s // c
    h = pl.program_id(1)

    def conv_silu(src_ref, cw_ref):
        xpad[0:SUBLANES, :] = jnp.zeros((SUBLANES, HEAD_DIM), F32)
        xpad[SUBLANES:SUBLANES + s, :] = src_ref[...].astype(F32)
        base = SUBLANES - (CONV_WIDTH - 1)
        acc = cw_ref[0:1, :] * xpad[base:base + s, :]
        for j in range(1, CONV_WIDTH):
            acc = acc + cw_ref[j:j + 1, :] * xpad[base + j:base + j + s, :]
        return acc * _sigmoid(acc)

    def l2n(x):
        return x * lax.rsqrt(jnp.sum(x * x, axis=-1, keepdims=True) + NORM_EPS)

    qn[...] = l2n(conv_silu(q_ref, cwq_ref)) * (HEAD_DIM ** -0.5)
    kn[...] = l2n(conv_silu(k_ref, cwk_ref))
    vn[...] = conv_silu(v_ref, cwv_ref)

    bg = bg_ref[...]
    lane = lax.broadcasted_iota(jnp.int32, bg.shape, 1)
    beta_col = jnp.sum(jnp.where(lane == h, bg, 0.0), axis=-1, keepdims=True)
    g_col = jnp.sum(jnp.where(lane == h + HEADS, bg, 0.0), axis=-1, keepdims=True)
    betab[...] = jnp.broadcast_to(beta_col, bg.shape)
    gb[...] = jnp.broadcast_to(g_col, bg.shape)

    row = lax.broadcasted_iota(jnp.int32, (c, c), 0)
    col = lax.broadcasted_iota(jnp.int32, (c, c), 1)
    incl = row >= col
    strict = row > col
    tril_b = incl.astype(BF16)

    def cumsum_rows(x):
        x1 = x.astype(BF16)
        r1 = x - x1.astype(F32)
        x2 = r1.astype(BF16)
        x3 = (r1 - x2.astype(F32)).astype(BF16)
        return _dot(tril_b, x1) + _dot(tril_b, x2) + _dot(tril_b, x3)

    def prep_group(ns):
        rows = [slice(n * c, (n + 1) * c) for n in ns]
        gcs = [cumsum_rows(gb[r, :]) for r in rows]
        decays = [jnp.exp(jnp.where(incl, gc - gc.T, NEG_INF)) for gc in gcs]
        a2s = [_dot_nt(jnp.concatenate([kn[r, :] * betab[r, :], qn[r, :]], axis=0).astype(BF16),
                       kn[r, :].astype(BF16)) for r in rows]
        lmats = [jnp.where(strict, a2[:c] * d, 0.0) for a2, d in zip(a2s, decays)]
        ts = _tri_inverse_group(lmats, row, col)
        uwbs = []
        for r, gc, t in zip(rows, gcs, ts):
            beta = betab[r, :]
            rhs = jnp.concatenate([kn[r, :] * beta * jnp.exp(gc), vn[r, :] * beta], axis=1)
            uwbs.append(_dot(t.astype(BF16), rhs.astype(BF16)).astype(BF16))
        for n, r, gc, a2, d, uwb in zip(ns, rows, gcs, a2s, decays, uwbs):
            g_last = gc[c - 1:c, :]
            intra = jnp.where(incl, a2[c:] * d, 0.0)
            kdec = (kn[r, :] * jnp.exp(g_last - gc)).astype(BF16)
            kwu = _dot_tn(kdec, uwb)
            iwu = _dot(intra.astype(BF16), uwb)
            xs[n] = kwu[:, :HEAD_DIM].astype(BF16)
            cs[n] = kwu[:, HEAD_DIM:]
            qps[n] = (qn[r, :] * jnp.exp(gc) - iwu[:, :HEAD_DIM]).astype(BF16)
            olocs[n] = iwu[:, HEAD_DIM:]
            gls[n] = jnp.broadcast_to(jnp.exp(g_last), (SUBLANES, HEAD_DIM))

    for grp in range(n_chunks // GDN_GROUP):
        prep_group([grp * GDN_GROUP + u for u in range(GDN_GROUP)])

    nw = nw_ref[...]
    state = jnp.zeros((HEAD_DIM, HEAD_DIM), F32)
    for n in range(n_chunks):
        rows = slice(n * c, (n + 1) * c)
        sb = state.astype(BF16)
        o = _dot(qps[n], sb) + olocs[n]
        o = o * lax.rsqrt(jnp.mean(o * o, axis=-1, keepdims=True) + NORM_EPS)
        o = o * nw * zs_ref[rows, :].astype(F32)
        o_ref[rows, :] = o.astype(BF16)
        state = state * gls[n][0:1, :] + cs[n] - _dot(xs[n], sb)


def _gdn(proj, conv_w, bg, norm_w, batch, seq):
    c = GDN_CHUNK
    n_chunks = seq // c
    blk = lambda off: pl.BlockSpec((seq, HEAD_DIM), lambda b, h: (b, off + h))
    cw = lambda off: pl.BlockSpec((CONV_WIDTH, HEAD_DIM), lambda b, h: (0, off + h))
    return pl.pallas_call(
        _gdn_kernel,
        grid=(batch, HEADS),
        in_specs=[
            blk(QKV_A), blk(QKV_A + HEADS), blk(QKV_A + 2 * HEADS),
            cw(0), cw(HEADS), cw(2 * HEADS),
            pl.BlockSpec((seq, LANES), lambda b, h: (b, 0)),
            blk(Z_A),
            pl.BlockSpec((1, HEAD_DIM), lambda b, h: (0, 0)),
        ],
        out_specs=pl.BlockSpec((seq, HEAD_DIM), lambda b, h: (b, h)),
        out_shape=jax.ShapeDtypeStruct((batch * seq, HEADS * HEAD_DIM), BF16),
        scratch_shapes=[
            pltpu.VMEM((seq + SUBLANES, HEAD_DIM), F32),
            pltpu.VMEM((seq, HEAD_DIM), F32),
            pltpu.VMEM((seq, HEAD_DIM), F32),
            pltpu.VMEM((seq, HEAD_DIM), F32),
            pltpu.VMEM((seq, LANES), F32),
            pltpu.VMEM((seq, LANES), F32),
            pltpu.VMEM((n_chunks, HEAD_DIM, HEAD_DIM), BF16),
            pltpu.VMEM((n_chunks, HEAD_DIM, HEAD_DIM), F32),
            pltpu.VMEM((n_chunks, c, HEAD_DIM), BF16),
            pltpu.VMEM((n_chunks, c, HEAD_DIM), F32),
            pltpu.VMEM((n_chunks, SUBLANES, HEAD_DIM), F32),
        ],
        compiler_params=pltpu.CompilerParams(
            dimension_semantics=("parallel", "parallel"), vmem_limit_bytes=VMEM_LIMIT),
        name="gdn",
    )(proj, proj, proj, conv_w, conv_w, conv_w, bg, proj, norm_w)


def _moba_kernel(q_ref, k_ref, v_ref, zs_ref, o_ref):
    s = q_ref.shape[0]
    blk = MOBA_BLOCK
    nb = s // blk

    prow = lax.broadcasted_iota(jnp.int32, (LANES, s), 0)
    pcol = lax.broadcasted_iota(jnp.int32, (LANES, s), 1)
    pind = jnp.where(prow == pcol // blk, 1.0 / blk, 0.0).astype(BF16)
    kmean = _dot(pind, k_ref[...])
    cand = 2 * SUBLANES
    km_full = kmean.astype(BF16)
    km_hi = km_full[0:cand, :]
    km_lo = (kmean - km_full.astype(F32)).astype(BF16)[0:cand, :]

    sub = lax.broadcasted_iota(jnp.int32, (cand, blk), 0)
    key_i = lax.broadcasted_iota(jnp.int32, (blk, blk), 0)
    qry_i = lax.broadcasted_iota(jnp.int32, (blk, blk), 1)
    causal = key_i <= qry_i
    v_t = [v_ref[j * blk:(j + 1) * blk, :].astype(F32).T.astype(BF16) for j in range(nb)]

    def score_stage(i):
        q = q_ref[i * blk:(i + 1) * blk, :]
        pieces = [_dot_nt(k_ref[j * blk:(j + 1) * blk, :], q) for j in range(i + 1)]
        gt = _dot_nt(km_hi, q) + _dot_nt(km_lo, q) if i > MOBA_TOPK else None
        return pieces, gt

    staged = score_stage(0)
    for i in range(nb):
        rows = slice(i * blk, (i + 1) * blk)
        pieces, gt = staged
        if i + 1 < nb:
            staged = score_stage(i + 1)
        if i > MOBA_TOPK:
            rank = jnp.zeros((cand, blk), jnp.int32)
            for jp in range(i):
                gj = jnp.broadcast_to(gt[jp:jp + 1, :], (cand, blk))
                ahead = jnp.logical_or(gj > gt, jnp.logical_and(gj == gt, sub > jp))
                rank = rank + ahead.astype(jnp.int32)
            bias = jnp.where(rank >= MOBA_TOPK, NEG_INF, 0.0)
            for j in range(i):
                pieces[j] = pieces[j] + bias[j:j + 1, :]
        pieces[i] = jnp.where(causal, pieces[i], NEG_INF)
        mt = pieces[0]
        for pc in pieces[1:]:
            mt = jnp.maximum(mt, pc)
        m = jnp.max(mt, axis=0, keepdims=True)
        pt = None
        o_t = jnp.zeros((HEAD_DIM, blk), F32)
        for j, pc in enumerate(pieces):
            p = jnp.exp2(pc - m)
            pt = p if pt is None else pt + p
            o_t = o_t + _dot(v_t[j], p.astype(BF16))
        l = jnp.sum(pt, axis=0, keepdims=True)
        o = (o_t / l).T
        o_ref[rows, :] = (o * zs_ref[rows, :].astype(F32)).astype(BF16)


def _moba(proj, batch, seq):
    blk = lambda off: pl.BlockSpec((seq, HEAD_DIM), lambda b, h: (b, off + h))
    return pl.pallas_call(
        _moba_kernel,
        grid=(batch, HEADS),
        in_specs=[blk(QKV_B), blk(QKV_B + HEADS), blk(QKV_B + 2 * HEADS), blk(Z_B)],
        out_specs=pl.BlockSpec((seq, HEAD_DIM), lambda b, h: (b, h)),
        out_shape=jax.ShapeDtypeStruct((batch * seq, HEADS * HEAD_DIM), BF16),
        compiler_params=pltpu.CompilerParams(
            dimension_semantics=("parallel", "parallel"), vmem_limit_bytes=VMEM_LIMIT),
        name="moba",
    )(proj, proj, proj, proj)


def _outproj_kernel(oa_ref, ob_ref, sa_ref, sb_ref, x_ref, wa_ref, wb_ref, wo_ref, pw_ref, o_ref):
    ua = _dot(oa_ref[...], wa_ref[...])
    ub = _dot(ob_ref[...], wb_ref[...])
    merged = sa_ref[...].astype(F32) * ua + sb_ref[...].astype(F32) * ub
    y = _dot(merged.astype(BF16), wo_ref[...])
    inv = lax.rsqrt(jnp.mean(y * y, axis=-1, keepdims=True) + NORM_EPS)
    o_ref[...] = x_ref[...] + y * inv * pw_ref[...]


def _outproj(oa, ob, proj, x2, wa, wb, wo, pw):
    m, d = x2.shape
    width = oa.shape[1]
    tm = OUT_TM
    const = lambda shape: pl.BlockSpec(shape, lambda i: (0, 0), pipeline_mode=pl.Buffered(1))
    return pl.pallas_call(
        _outproj_kernel,
        grid=(m // tm,),
        in_specs=[
            pl.BlockSpec((tm, width), lambda i: (i, 0)),
            pl.BlockSpec((tm, width), lambda i: (i, 0)),
            pl.BlockSpec((tm, d), lambda i: (i, GATE_A_BLK2048)),
            pl.BlockSpec((tm, d), lambda i: (i, GATE_B_BLK2048)),
            pl.BlockSpec((tm, d), lambda i: (i, 0)),
            const((width, d)), const((width, d)), const((d, d)), const((1, d)),
        ],
        out_specs=pl.BlockSpec((tm, d), lambda i: (i, 0)),
        out_shape=jax.ShapeDtypeStruct((m, d), F32),
        compiler_params=pltpu.CompilerParams(
            dimension_semantics=("parallel",), vmem_limit_bytes=VMEM_LIMIT),
        name="outproj",
    )(oa, ob, proj, proj, x2, wa, wb, wo, pw)


def _layer(x, pre_w, w_in, conv_w, a_log, dt_bias, gdn_norm_w, w_a, w_b, w_out, post_w):
    batch, seq, d = x.shape
    width = HEADS * HEAD_DIM
    ba_lo = 4 * width
    ba_hi = ba_lo + 2 * HEADS
    x2 = x.reshape(batch * seq, d)
    w_bf = w_in.astype(BF16)
    w_lo = w_bf
    w_hi = w_bf[:, ba_hi:]
    w_ba = jnp.pad(w_bf[:, ba_lo:ba_hi], ((0, 0), (0, LANES - 2 * HEADS)))
    pad_row = lambda v: jnp.pad(v.astype(F32), (HEADS, LANES - 2 * HEADS)).reshape(1, LANES)
    proj, bg = _inproj(x2, pre_w.reshape(1, d), w_lo, w_hi, w_ba, pad_row(a_log), pad_row(dt_bias))
    oa = _gdn(proj, conv_w, bg, gdn_norm_w.reshape(1, HEAD_DIM), batch, seq)
    ob = _moba(proj, batch, seq)
    out = _outproj(oa, ob, proj, x2, w_a.astype(BF16), w_b.astype(BF16), w_out.astype(BF16),
                   post_w.reshape(1, d))
    return out.reshape(batch, seq, d)


def kernel(x, pre_norm_w, w_in, conv_w, a_log, dt_bias, gdn_norm_w, w_branch_a, w_branch_b, w_out,
           post_norm_w):
    depth = pre_norm_w.shape[0]
    for l in range(depth):
        x = _layer(x, pre_norm_w[l], w_in[l], conv_w[l], a_log[l], dt_bias[l], gdn_norm_w[l],
                   w_branch_a[l], w_branch_b[l], w_out[l], post_norm_w[l])
    return x
```

```python
import functools

import jax
import jax.numpy as jnp
from jax import lax
from jax.experimental import pallas as pl
from jax.experimental.pallas import tpu as pltpu

F32 = jnp.float32
BF16 = jnp.bfloat16

NORM_EPS = 1e-6
NEG_INF = -1e30

HEADS = 8
HEAD_DIM = 128
CONV_WIDTH = 4
MOBA_BLOCK = 256
MOBA_TOPK = 3
GDN_CHUNK = 128
GDN_GROUP = 16

LANES = 128
SUBLANES = 8

QKV_A, Z_A, QKV_B, Z_B = 0, 24, 32, 56
GATE_A_BLK2048, GATE_B_BLK2048 = 4, 5

INPROJ_TM = 1024
INPROJ_TN = 1024
TILE_BLKS = INPROJ_TN // LANES
N_LO_TILES = (4 * HEADS * HEAD_DIM) // INPROJ_TN
OUT_TM = 256

MOBA_QSCALE = (HEAD_DIM ** -0.5) * 1.4426950408889634

VMEM_LIMIT = 56 * 1024 * 1024


def _dot(a, b):
    return jnp.dot(a, b, preferred_element_type=F32)


def _dot_nt(a, b):
    return lax.dot_general(a, b, (((1,), (1,)), ((), ())), preferred_element_type=F32)


def _dot_tn(a, b):
    return lax.dot_general(a, b, (((0,), (0,)), ((), ())), preferred_element_type=F32)


def _sigmoid(x):
    return 0.5 * jnp.tanh(0.5 * x) + 0.5


def _softplus(x):
    return jnp.maximum(x, 0.0) + jnp.log1p(jnp.exp(-jnp.abs(x)))


def _inproj_kernel(x_ref, nw_ref, w_ref, wba_ref, alog_ref, dtb_ref, proj_ref, bg_ref, h_ref):
    j = pl.program_id(1)

    @pl.when(j == 0)
    def _():
        xf = x_ref[...]
        inv = lax.rsqrt(jnp.mean(xf * xf, axis=-1, keepdims=True) + NORM_EPS)
        h_ref[...] = (xf * inv * nw_ref[...]).astype(BF16)
        ba = _dot_nt(h_ref[...], wba_ref[...])
        lane = lax.broadcasted_iota(jnp.int32, ba.shape, 1)
        beta = _sigmoid(ba)
        g = -jnp.exp(alog_ref[...]) * _softplus(ba + dtb_ref[...])
        bg_ref[...] = jnp.where(lane < HEADS, beta, g)

    is_silu = jnp.logical_or(j == Z_A // TILE_BLKS, j == Z_B // TILE_BLKS)
    is_sig = j >= GATE_A_BLK2048 * 2
    lin = jnp.where(j == QKV_B // TILE_BLKS, MOBA_QSCALE, 1.0)

    acc = _dot_nt(h_ref[...], w_ref[...])

    @pl.when(jnp.logical_not(jnp.logical_or(is_silu, is_sig)))
    def _():
        proj_ref[...] = (acc * lin).astype(BF16)

    @pl.when(is_silu)
    def _():
        proj_ref[...] = (acc * _sigmoid(acc)).astype(BF16)

    @pl.when(is_sig)
    def _():
        proj_ref[...] = _sigmoid(acc).astype(BF16)


def _inproj(x2, nw, w_t, w_ba, alog_row, dtb_row):
    m, d = x2.shape
    tm, tn = INPROJ_TM, INPROJ_TN
    n = w_t.shape[0] - 2 * HEADS
    assert n % tn == 0 and N_LO_TILES * tn == 4 * HEADS * HEAD_DIM

    def w_rows(i, j):
        return (pl.multiple_of(j * tn + jnp.where(j >= N_LO_TILES, 2 * HEADS, 0), 2 * HEADS), 0)

    return pl.pallas_call(
        _inproj_kernel,
        grid=(m // tm, n // tn),
        in_specs=[
            pl.BlockSpec((tm, d), lambda i, j: (i, 0)),
            pl.BlockSpec((1, d), lambda i, j: (0, 0)),
            pl.BlockSpec((pl.Element(tn), pl.Element(d)), w_rows),
            pl.BlockSpec((LANES, d), lambda i, j: (0, 0)),
            pl.BlockSpec((1, LANES), lambda i, j: (0, 0)),
            pl.BlockSpec((1, LANES), lambda i, j: (0, 0)),
        ],
        out_specs=[
            pl.BlockSpec((tm, tn), lambda i, j: (i, j)),
            pl.BlockSpec((tm, LANES), lambda i, j: (i, 0)),
        ],
        out_shape=[
            jax.ShapeDtypeStruct((m, n), BF16),
            jax.ShapeDtypeStruct((m, LANES), F32),
        ],
        scratch_shapes=[pltpu.VMEM((tm, d), BF16)],
        compiler_params=pltpu.CompilerParams(
            dimension_semantics=("parallel", "arbitrary"), vmem_limit_bytes=VMEM_LIMIT),
        name="inproj",
    )(x2, nw, w_t, w_ba, alog_row, dtb_row)


def _tri_inverse_group(lmats, row, col):
    c = lmats[0].shape[0]

    def sub_mask(s_log):
        bi = jnp.right_shift(row, s_log)
        bj = jnp.right_shift(col, s_log)
        return jnp.logical_and(jnp.bitwise_xor(bi, bj) == 1, jnp.bitwise_and(bi, 1) == 1)

    eye = (row == col).astype(F32)
    m0 = sub_mask(0)
    tds = [eye - jnp.where(m0, lm, 0.0) for lm in lmats]
    s_log = 1
    while (1 << s_log) < c:
        ms = sub_mask(s_log)
        tbs = [td.astype(BF16) for td in tds]
        x1s = [_dot(tb, jnp.where(ms, lm, 0.0).astype(BF16)).astype(BF16)
               for tb, lm in zip(tbs, lmats)]
        tds = [td - _dot(x1, tb) for td, x1, tb in zip(tds, x1s, tbs)]
        s_log += 1
    return tds


def _gdn_kernel(q_ref, k_ref, v_ref, cwq_ref, cwk_ref, cwv_ref, bg_ref, zs_ref, nw_ref, o_ref,
                xpad, qn, kn, vn, betab, gb, xs, cs, qps, olocs, gls):
    s = q_ref.shape[0]
    c = GDN_CHUNK
    n_chunks = s // c
    h = pl.program_id(1)

    def conv_silu(src_ref, cw_ref):
        xpad[0:SUBLANES, :] = jnp.zeros((SUBLANES, HEAD_DIM), F32)
        xpad[SUBLANES:SUBLANES + s, :] = src_ref[...].astype(F32)
        base = SUBLANES - (CONV_WIDTH - 1)
        acc = cw_ref[0:1, :] * xpad[base:base + s, :]
        for j in range(1, CONV_WIDTH):
            acc = acc + cw_ref[j:j + 1, :] * xpad[base + j:base + j + s, :]
        return acc * _sigmoid(acc)

    def l2n(x):
        return x * lax.rsqrt(jnp.sum(x * x, axis=-1, keepdims=True) + NORM_EPS)

    qn[...] = l2n(conv_silu(q_ref, cwq_ref)) * (HEAD_DIM ** -0.5)
    kn[...] = l2n(conv_silu(k_ref, cwk_ref))
    vn[...] = conv_silu(v_ref, cwv_ref)

    bg = bg_ref[...]
    lane = lax.broadcasted_iota(jnp.int32, bg.shape, 1)
    beta_col = jnp.sum(jnp.where(lane == h, bg, 0.0), axis=-1, keepdims=True)
    g_col = jnp.sum(jnp.where(lane == h + HEADS, bg, 0.0), axis=-1, keepdims=True)
    betab[...] = jnp.broadcast_to(beta_col, bg.shape)
    gb[...] = jnp.broadcast_to(g_col, bg.shape)

    row = lax.broadcasted_iota(jnp.int32, (c, c), 0)
    col = lax.broadcasted_iota(jnp.int32, (c, c), 1)
    incl = row >= col
    strict = row > col
    tril_b = incl.astype(BF16)

    def cumsum_rows(x):
        x1 = x.astype(BF16)
        r1 = x - x1.astype(F32)
        x2 = r1.astype(BF16)
        x3 = (r1 - x2.astype(F32)).astype(BF16)
        return _dot(tril_b, x1) + _dot(tril_b, x2) + _dot(tril_b, x3)

    def prep_group(ns):
        rows = [slice(n * c, (n + 1) * c) for n in ns]
        gcs = [cumsum_rows(gb[r, :]) for r in rows]
        decays = [jnp.exp(jnp.where(incl, gc - gc.T, NEG_INF)) for gc in gcs]
        a2s = [_dot_nt(jnp.concatenate([kn[r, :] * betab[r, :], qn[r, :]], axis=0).astype(BF16),
                       kn[r, :].astype(BF16)) for r in rows]
        lmats = [jnp.where(strict, a2[:c] * d, 0.0) for a2, d in zip(a2s, decays)]
        ts = _tri_inverse_group(lmats, row, col)
        uwbs = []
        for r, gc, t in zip(rows, gcs, ts):
            beta = betab[r, :]
            rhs = jnp.concatenate([kn[r, :] * beta * jnp.exp(gc), vn[r, :] * beta], axis=1)
            uwbs.append(_dot(t.astype(BF16), rhs.astype(BF16)).astype(BF16))
        for n, r, gc, a2, d, uwb in zip(ns, rows, gcs, a2s, decays, uwbs):
            g_last = gc[c - 1:c, :]
            intra = jnp.where(incl, a2[c:] * d, 0.0)
            kdec = (kn[r, :] * jnp.exp(g_last - gc)).astype(BF16)
            kwu = _dot_tn(kdec, uwb)
            iwu = _dot(intra.astype(BF16), uwb)
            xs[n] = kwu[:, :HEAD_DIM].astype(BF16)
            cs[n] = kwu[:, HEAD_DIM:]
            qps[n] = (qn[r, :] * jnp.exp(gc) - iwu[:, :HEAD_DIM]).astype(BF16)
            olocs[n] = iwu[:, HEAD_DIM:]
            gls[n] = jnp.broadcast_to(jnp.exp(g_last), (SUBLANES, HEAD_DIM))

    for grp in range(n_chunks // GDN_GROUP):
        prep_group([grp * GDN_GROUP + u for u in range(GDN_GROUP)])

    nw = nw_ref[...]
    state = jnp.zeros((HEAD_DIM, HEAD_DIM), F32)
    for n in range(n_chunks):
        rows = slice(n * c, (n + 1) * c)
        sb = state.astype(BF16)
        o = _dot(qps[n], sb) + olocs[n]
        o = o * lax.rsqrt(jnp.mean(o * o, axis=-1, keepdims=True) + NORM_EPS)
        o = o * nw * zs_ref[rows, :].astype(F32)
        o_ref[rows, :] = o.astype(BF16)
        state = state * gls[n][0:1, :] + cs[n] - _dot(xs[n], sb)


def _gdn(proj, conv_w, bg, norm_w, batch, seq):
    c = GDN_CHUNK
    n_chunks = seq // c
    blk = lambda off: pl.BlockSpec((seq, HEAD_DIM), lambda b, h: (b, off + h))
    cw = lambda off: pl.BlockSpec((CONV_WIDTH, HEAD_DIM), lambda b, h: (0, off + h))
    return pl.pallas_call(
        _gdn_kernel,
        grid=(batch, HEADS),
        in_specs=[
            blk(QKV_A), blk(QKV_A + HEADS), blk(QKV_A + 2 * HEADS),
            cw(0), cw(HEADS), cw(2 * HEADS),
            pl.BlockSpec((seq, LANES), lambda b, h: (b, 0)),
            blk(Z_A),
            pl.BlockSpec((1, HEAD_DIM), lambda b, h: (0, 0)),
        ],
        out_specs=pl.BlockSpec((seq, HEAD_DIM), lambda b, h: (b, h)),
        out_shape=jax.ShapeDtypeStruct((batch * seq, HEADS * HEAD_DIM), BF16),
        scratch_shapes=[
            pltpu.VMEM((seq + SUBLANES, HEAD_DIM), F32),
            pltpu.VMEM((seq, HEAD_DIM), F32),
            pltpu.VMEM((seq, HEAD_DIM), F32),
            pltpu.VMEM((seq, HEAD_DIM), F32),
            pltpu.VMEM((seq, LANES), F32),
            pltpu.VMEM((seq, LANES), F32),
            pltpu.VMEM((n_chunks, HEAD_DIM, HEAD_DIM), BF16),
            pltpu.VMEM((n_chunks, HEAD_DIM, HEAD_DIM), F32),
            pltpu.VMEM((n_chunks, c, HEAD_DIM), BF16),
            pltpu.VMEM((n_chunks, c, HEAD_DIM), F32),
            pltpu.VMEM((n_chunks, SUBLANES, HEAD_DIM), F32),
        ],
        compiler_params=pltpu.CompilerParams(
            dimension_semantics=("parallel", "parallel"), vmem_limit_bytes=VMEM_LIMIT),
        name="gdn",
    )(proj, proj, proj, conv_w, conv_w, conv_w, bg, proj, norm_w)


def _moba_kernel(q_ref, k_ref, v_ref, zs_ref, o_ref):
    s = q_ref.shape[0]
    blk = MOBA_BLOCK
    nb = s // blk

    prow = lax.broadcasted_iota(jnp.int32, (LANES, s), 0)
    pcol = lax.broadcasted_iota(jnp.int32, (LANES, s), 1)
    pind = jnp.where(prow == pcol // blk, 1.0 / blk, 0.0).astype(BF16)
    kmean = _dot(pind, k_ref[...])
    cand = 2 * SUBLANES
    km_full = kmean.astype(BF16)
    km_hi = km_full[0:cand, :]
    km_lo = (kmean - km_full.astype(F32)).astype(BF16)[0:cand, :]

    sub = lax.broadcasted_iota(jnp.int32, (cand, blk), 0)
    key_i = lax.broadcasted_iota(jnp.int32, (blk, blk), 0)
    qry_i = lax.broadcasted_iota(jnp.int32, (blk, blk), 1)
    causal = key_i <= qry_i
    v_t = [v_ref[j * blk:(j + 1) * blk, :].astype(F32).T.astype(BF16) for j in range(nb)]

    def score_stage(i):
        q = q_ref[i * blk:(i + 1) * blk, :]
        pieces = [_dot_nt(k_ref[j * blk:(j + 1) * blk, :], q) for j in range(i + 1)]
        gt = _dot_nt(km_hi, q) + _dot_nt(km_lo, q) if i > MOBA_TOPK else None
        return pieces, gt

    staged = score_stage(0)
    for i in range(nb):
        rows = slice(i * blk, (i + 1) * blk)
        pieces, gt = staged
        if i + 1 < nb:
            staged = score_stage(i + 1)
        if i > MOBA_TOPK:
            rank = jnp.zeros((cand, blk), jnp.int32)
            for jp in range(i):
                gj = jnp.broadcast_to(gt[jp:jp + 1, :], (cand, blk))
                ahead = jnp.logical_or(gj > gt, jnp.logical_and(gj == gt, sub > jp))
                rank = rank + ahead.astype(jnp.int32)
            bias = jnp.where(rank >= MOBA_TOPK, NEG_INF, 0.0)
            for j in range(i):
                pieces[j] = pieces[j] + bias[j:j + 1, :]
        pieces[i] = jnp.where(causal, pieces[i], NEG_INF)
        mt = pieces[0]
        for pc in pieces[1:]:
            mt = jnp.maximum(mt, pc)
        m = jnp.max(mt, axis=0, keepdims=True)
        pt = None
        o_t = jnp.zeros((HEAD_DIM, blk), F32)
        for j, pc in enumerate(pieces):
            p = jnp.exp2(pc - m)
            pt = p if pt is None else pt + p
            o_t = o_t + _dot(v_t[j], p.astype(BF16))
        l = jnp.sum(pt, axis=0, keepdims=True)
        o = (o_t / l).T
        o_ref[rows, :] = (o * zs_ref[rows, :].astype(F32)).astype(BF16)


def _moba(proj, batch, seq):
    blk = lambda off: pl.BlockSpec((seq, HEAD_DIM), lambda b, h: (b, off + h))
    return pl.pallas_call(
        _moba_kernel,
        grid=(batch, HEADS),
        in_specs=[blk(QKV_B), blk(QKV_B + HEADS), blk(QKV_B + 2 * HEADS), blk(Z_B)],
        out_specs=pl.BlockSpec((seq, HEAD_DIM), lambda b, h: (b, h)),
        out_shape=jax.ShapeDtypeStruct((batch * seq, HEADS * HEAD_DIM), BF16),
        compiler_params=pltpu.CompilerParams(
            dimension_semantics=("parallel", "parallel"), vmem_limit_bytes=VMEM_LIMIT),
        name="moba",
    )(proj, proj, proj, proj)


def _outproj_kernel(oa_ref, ob_ref, sa_ref, sb_ref, x_ref, wa_ref, wb_ref, wo_ref, pw_ref, o_ref):
    ua = _dot(oa_ref[...], wa_ref[...])
    ub = _dot(ob_ref[...], wb_ref[...])
    merged = sa_ref[...].astype(F32) * ua + sb_ref[...].astype(F32) * ub
    y = _dot(merged.astype(BF16), wo_ref[...])
    inv = lax.rsqrt(jnp.mean(y * y, axis=-1, keepdims=True) + NORM_EPS)
    o_ref[...] = x_ref[...] + y * inv * pw_ref[...]


def _outproj(oa, ob, proj, x2, wa, wb, wo, pw):
    m, d = x2.shape
    width = oa.shape[1]
    tm = OUT_TM
    const = lambda shape: pl.BlockSpec(shape, lambda i: (0, 0), pipeline_mode=pl.Buffered(1))
    return pl.pallas_call(
        _outproj_kernel,
        grid=(m // tm,),
        in_specs=[
            pl.BlockSpec((tm, width), lambda i: (i, 0)),
            pl.BlockSpec((tm, width), lambda i: (i, 0)),
            pl.BlockSpec((tm, d), lambda i: (i, GATE_A_BLK2048)),
            pl.BlockSpec((tm, d), lambda i: (i, GATE_B_BLK2048)),
            pl.BlockSpec((tm, d), lambda i: (i, 0)),
            const((width, d)), const((width, d)), const((d, d)), const((1, d)),
        ],
        out_specs=pl.BlockSpec((tm, d), lambda i: (i, 0)),
        out_shape=jax.ShapeDtypeStruct((m, d), F32),
        compiler_params=pltpu.CompilerParams(
            dimension_semantics=("parallel",), vmem_limit_bytes=VMEM_LIMIT),
        name="outproj",
    )(oa, ob, proj, proj, x2, wa, wb, wo, pw)


def _layer(x, pre_w, w_in, conv_w, a_log, dt_bias, gdn_norm_w, w_a, w_b, w_out, post_w):
    batch, seq, d = x.shape
    width = HEADS * HEAD_DIM
    ba_lo = 4 * width
    ba_hi = ba_lo + 2 * HEADS
    x2 = x.reshape(batch * seq, d)
    w_t = jnp.swapaxes(w_in, 0, 1).astype(BF16)
    w_ba = jnp.pad(w_t[ba_lo:ba_hi, :], ((0, LANES - 2 * HEADS), (0, 0)))
    pad_row = lambda v: jnp.pad(v.astype(F32), (HEADS, LANES - 2 * HEADS)).reshape(1, LANES)
    proj, bg = _inproj(x2, pre_w.reshape(1, d), w_t, w_ba, pad_row(a_log), pad_row(dt_bias))
    oa = _gdn(proj, conv_w, bg, gdn_norm_w.reshape(1, HEAD_DIM), batch, seq)
    ob = _moba(proj, batch, seq)
    out = _outproj(oa, ob, proj, x2, w_a.astype(BF16), w_b.astype(BF16), w_out.astype(BF16),
                   post_w.reshape(1, d))
    return out.reshape(batch, seq, d)


def kernel(x, pre_norm_w, w_in, conv_w, a_log, dt_bias, gdn_norm_w, w_branch_a, w_branch_b, w_out,
           post_norm_w):
    depth = pre_norm_w.shape[0]
    for l in range(depth):
        x = _layer(x, pre_norm_w[l], w_in[l], conv_w[l], a_log[l], dt_bias[l], gdn_norm_w[l],
                   w_branch_a[l], w_branch_b[l], w_out[l], post_norm_w[l])
    return x
```

```python
import functools

import jax
import jax.numpy as jnp
from jax import lax
from jax.experimental import pallas as pl
from jax.experimental.pallas import tpu as pltpu

F32 = jnp.float32
BF16 = jnp.bfloat16

NORM_EPS = 1e-6
NEG_INF = -1e30

HEADS = 8
HEAD_DIM = 128
CONV_WIDTH = 4
MOBA_BLOCK = 256
MOBA_TOPK = 3
GDN_CHUNK = 128

LANES = 128
SUBLANES = 8

QKV_A, Z_A, QKV_B, Z_B = 0, 24, 32, 56
GATE_A_BLK2048, GATE_B_BLK2048 = 4, 5

INPROJ_TM = 1024
INPROJ_TN = 1024
TILE_BLKS = INPROJ_TN // LANES
N_LO_TILES = (4 * HEADS * HEAD_DIM) // INPROJ_TN
OUT_TM = 512
OUT_SPLIT = 4

MOBA_QSCALE = (HEAD_DIM ** -0.5) * 1.4426950408889634

VMEM_LIMIT = 56 * 1024 * 1024


def _dot(a, b):
    return jnp.dot(a, b, preferred_element_type=F32)


def _dot_nt(a, b):
    return lax.dot_general(a, b, (((1,), (1,)), ((), ())), preferred_element_type=F32)


def _dot_tn(a, b):
    return lax.dot_general(a, b, (((0,), (0,)), ((), ())), preferred_element_type=F32)


def _sigmoid(x):
    return 0.5 * jnp.tanh(0.5 * x) + 0.5


def _softplus(x):
    return jnp.maximum(x, 0.0) + jnp.log1p(jnp.exp(-jnp.abs(x)))


def _inproj_kernel(x_ref, nw_ref, w_ref, wba_ref, alog_ref, dtb_ref, proj_ref, bg_ref, h_ref):
    j = pl.program_id(1)

    @pl.when(j == 0)
    def _():
        xf = x_ref[...]
        inv = lax.rsqrt(jnp.mean(xf * xf, axis=-1, keepdims=True) + NORM_EPS)
        h_ref[...] = (xf * inv * nw_ref[...]).astype(BF16)
        ba = _dot_nt(h_ref[...], wba_ref[...])
        lane = lax.broadcasted_iota(jnp.int32, ba.shape, 1)
        beta = _sigmoid(ba)
        g = -jnp.exp(alog_ref[...]) * _softplus(ba + dtb_ref[...])
        bg_ref[...] = jnp.where(lane < HEADS, beta, g)

    is_silu = jnp.logical_or(j == Z_A // TILE_BLKS, j == Z_B // TILE_BLKS)
    is_sig = j >= GATE_A_BLK2048 * 2
    lin = jnp.where(j == QKV_B // TILE_BLKS, MOBA_QSCALE, 1.0)

    acc = _dot_nt(h_ref[...], w_ref[...])

    @pl.when(jnp.logical_not(jnp.logical_or(is_silu, is_sig)))
    def _():
        proj_ref[...] = (acc * lin).astype(BF16)

    @pl.when(is_silu)
    def _():
        proj_ref[...] = (acc * _sigmoid(acc)).astype(BF16)

    @pl.when(is_sig)
    def _():
        proj_ref[...] = _sigmoid(acc).astype(BF16)


def _inproj(x2, nw, w_t, w_ba, alog_row, dtb_row):
    m, d = x2.shape
    tm, tn = INPROJ_TM, INPROJ_TN
    n = w_t.shape[0] - 2 * HEADS
    assert n % tn == 0 and N_LO_TILES * tn == 4 * HEADS * HEAD_DIM

    def w_rows(i, j):
        return (pl.multiple_of(j * tn + jnp.where(j >= N_LO_TILES, 2 * HEADS, 0), 2 * HEADS), 0)

    return pl.pallas_call(
        _inproj_kernel,
        grid=(m // tm, n // tn),
        in_specs=[
            pl.BlockSpec((tm, d), lambda i, j: (i, 0)),
            pl.BlockSpec((1, d), lambda i, j: (0, 0)),
            pl.BlockSpec((pl.Element(tn), pl.Element(d)), w_rows),
            pl.BlockSpec((LANES, d), lambda i, j: (0, 0)),
            pl.BlockSpec((1, LANES), lambda i, j: (0, 0)),
            pl.BlockSpec((1, LANES), lambda i, j: (0, 0)),
        ],
        out_specs=[
            pl.BlockSpec((tm, tn), lambda i, j: (i, j)),
            pl.BlockSpec((tm, LANES), lambda i, j: (i, 0)),
        ],
        out_shape=[
            jax.ShapeDtypeStruct((m, n), BF16),
            jax.ShapeDtypeStruct((m, LANES), F32),
        ],
        scratch_shapes=[pltpu.VMEM((tm, d), BF16)],
        compiler_params=pltpu.CompilerParams(
            dimension_semantics=("parallel", "arbitrary"), vmem_limit_bytes=VMEM_LIMIT),
        name="inproj",
    )(x2, nw, w_t, w_ba, alog_row, dtb_row)


def _interleave(*stages):
    live = list(stages)
    while live:
        for g in list(live):
            try:
                next(g)
            except StopIteration:
                live.remove(g)


def _tri_inverse_group(lmats, row, col, out):
    c = lmats[0].shape[0]

    def sub_mask(s_log):
        bi = jnp.right_shift(row, s_log)
        bj = jnp.right_shift(col, s_log)
        return jnp.logical_and(jnp.bitwise_xor(bi, bj) == 1, jnp.bitwise_and(bi, 1) == 1)

    eye = (row == col).astype(F32)
    m0 = sub_mask(0)
    tds = [eye - jnp.where(m0, lm, 0.0) for lm in lmats]
    s_log = 1
    while (1 << s_log) < c:
        ms = sub_mask(s_log)
        tbs = [td.astype(BF16) for td in tds]
        x1s = [_dot(tb, jnp.where(ms, lm, 0.0).astype(BF16)).astype(BF16)
               for tb, lm in zip(tbs, lmats)]
        yield
        tds = [td - _dot(x1, tb) for td, x1, tb in zip(tds, x1s, tbs)]
        yield
        s_log += 1
    out.extend(tds)


def _gdn_kernel(q_ref, k_ref, v_ref, cwq_ref, cwk_ref, cwv_ref, bg_ref, zs_ref, nw_ref, o_ref,
                xpad, qn, kn, vn, betab, gb, xs, cs, qps, olocs, gls):
    s = q_ref.shape[0]
    c = GDN_CHUNK
    n_chunks = s // c
    t = pl.program_id(0)
    n_heads_total = pl.num_programs(0) - 1
    h = lax.rem(jnp.minimum(t, n_heads_total - 1), HEADS)
    cur = lax.rem(t, 2)
    prv = 1 - cur

    @pl.when(t == 0)
    def _():
        xs[1] = jnp.zeros(xs.shape[1:], xs.dtype)
        cs[1] = jnp.zeros(cs.shape[1:], cs.dtype)
        qps[1] = jnp.zeros(qps.shape[1:], qps.dtype)
        olocs[1] = jnp.zeros(olocs.shape[1:], olocs.dtype)
        gls[1] = jnp.zeros(gls.shape[1:], gls.dtype)

    def conv_silu(src_ref, cw_ref):
        xpad[0:SUBLANES, :] = jnp.zeros((SUBLANES, HEAD_DIM), F32)
        xpad[SUBLANES:SUBLANES + s, :] = src_ref[...].astype(F32)
        base = SUBLANES - (CONV_WIDTH - 1)
        acc = cw_ref[0:1, :] * xpad[base:base + s, :]
        for j in range(1, CONV_WIDTH):
            acc = acc + cw_ref[j:j + 1, :] * xpad[base + j:base + j + s, :]
        return acc * _sigmoid(acc)

    def l2n(x):
        return x * lax.rsqrt(jnp.sum(x * x, axis=-1, keepdims=True) + NORM_EPS)

    qn[...] = l2n(conv_silu(q_ref, cwq_ref)) * (HEAD_DIM ** -0.5)
    kn[...] = l2n(conv_silu(k_ref, cwk_ref))
    vn[...] = conv_silu(v_ref, cwv_ref)

    bg = bg_ref[...]
    lane = lax.broadcasted_iota(jnp.int32, bg.shape, 1)
    beta_col = jnp.sum(jnp.where(lane == h, bg, 0.0), axis=-1, keepdims=True)
    g_col = jnp.sum(jnp.where(lane == h + HEADS, bg, 0.0), axis=-1, keepdims=True)
    betab[...] = jnp.broadcast_to(beta_col, bg.shape)
    gb[...] = jnp.broadcast_to(g_col, bg.shape)

    row = lax.broadcasted_iota(jnp.int32, (c, c), 0)
    col = lax.broadcasted_iota(jnp.int32, (c, c), 1)
    incl = row >= col
    strict = row > col
    tril_b = incl.astype(BF16)

    def cumsum_rows(x):
        x1 = x.astype(BF16)
        r1 = x - x1.astype(F32)
        x2 = r1.astype(BF16)
        x3 = (r1 - x2.astype(F32)).astype(BF16)
        return _dot(tril_b, x1) + _dot(tril_b, x2) + _dot(tril_b, x3)

    def prep_stage(ns):
        rows = [slice(n * c, (n + 1) * c) for n in ns]
        gcs = [cumsum_rows(gb[r, :]) for r in rows]
        yield
        decays = [jnp.exp(jnp.where(incl, gc - gc.T, NEG_INF)) for gc in gcs]
        a2s = [_dot_nt(jnp.concatenate([kn[r, :] * betab[r, :], qn[r, :]], axis=0).astype(BF16),
                       kn[r, :].astype(BF16)) for r in rows]
        yield
        lmats = [jnp.where(strict, a2[:c] * d, 0.0) for a2, d in zip(a2s, decays)]
        ts = []
        yield from _tri_inverse_group(lmats, row, col, ts)
        uwbs = []
        for r, gc, tinv in zip(rows, gcs, ts):
            beta = betab[r, :]
            rhs = jnp.concatenate([kn[r, :] * beta * jnp.exp(gc), vn[r, :] * beta], axis=1)
            uwbs.append(_dot(tinv.astype(BF16), rhs.astype(BF16)).astype(BF16))
        yield
        for n, r, gc, a2, d, uwb in zip(ns, rows, gcs, a2s, decays, uwbs):
            g_last = gc[c - 1:c, :]
            intra = jnp.where(incl, a2[c:] * d, 0.0)
            kdec = (kn[r, :] * jnp.exp(g_last - gc)).astype(BF16)
            kwu = _dot_tn(kdec, uwb)
            iwu = _dot(intra.astype(BF16), uwb)
            xs[cur, n] = kwu[:, :HEAD_DIM].astype(BF16)
            cs[cur, n] = kwu[:, HEAD_DIM:]
            qps[cur, n] = (qn[r, :] * jnp.exp(gc) - iwu[:, :HEAD_DIM]).astype(BF16)
            olocs[cur, n] = iwu[:, HEAD_DIM:]
            gls[cur, n] = jnp.broadcast_to(jnp.exp(g_last), (SUBLANES, HEAD_DIM))
            if n % 4 == 3:
                yield

    def recurrence_stage():
        nw = nw_ref[...]
        state = jnp.zeros((HEAD_DIM, HEAD_DIM), F32)
        for n in range(n_chunks):
            rows = slice(n * c, (n + 1) * c)
            sb = state.astype(BF16)
            o = _dot(qps[prv, n], sb) + olocs[prv, n]
            o = o * lax.rsqrt(jnp.mean(o * o, axis=-1, keepdims=True) + NORM_EPS)
            o = o * nw * zs_ref[rows, :].astype(F32)
            o_ref[rows, :] = o.astype(BF16)
            state = state * gls[prv, n][0:1, :] + cs[prv, n] - _dot(xs[prv, n], sb)
            yield

    _interleave(prep_stage(list(range(n_chunks))), recurrence_stage())


def _gdn(proj, conv_w, bg, norm_w, batch, seq):
    c = GDN_CHUNK
    n_chunks = seq // c
    n_heads_total = batch * HEADS
    rd = lambda t: jnp.minimum(t, n_heads_total - 1)
    wr = lambda t: jnp.maximum(t - 1, 0)
    blk = lambda off, hd: pl.BlockSpec(
        (seq, HEAD_DIM), lambda t: (hd(t) // HEADS, off + lax.rem(hd(t), HEADS)))
    cw = lambda off: pl.BlockSpec(
        (CONV_WIDTH, HEAD_DIM), lambda t: (0, off + lax.rem(rd(t), HEADS)))
    return pl.pallas_call(
        _gdn_kernel,
        grid=(n_heads_total + 1,),
        in_specs=[
            blk(QKV_A, rd), blk(QKV_A + HEADS, rd), blk(QKV_A + 2 * HEADS, rd),
            cw(0), cw(HEADS), cw(2 * HEADS),
            pl.BlockSpec((seq, LANES), lambda t: (rd(t) // HEADS, 0)),
            blk(Z_A, wr),
            pl.BlockSpec((1, HEAD_DIM), lambda t: (0, 0)),
        ],
        out_specs=blk(0, wr),
        out_shape=jax.ShapeDtypeStruct((batch * seq, HEADS * HEAD_DIM), BF16),
        scratch_shapes=[
            pltpu.VMEM((seq + SUBLANES, HEAD_DIM), F32),
            pltpu.VMEM((seq, HEAD_DIM), F32),
            pltpu.VMEM((seq, HEAD_DIM), F32),
            pltpu.VMEM((seq, HEAD_DIM), F32),
            pltpu.VMEM((seq, LANES), F32),
            pltpu.VMEM((seq, LANES), F32),
            pltpu.VMEM((2, n_chunks, HEAD_DIM, HEAD_DIM), BF16),
            pltpu.VMEM((2, n_chunks, HEAD_DIM, HEAD_DIM), F32),
            pltpu.VMEM((2, n_chunks, c, HEAD_DIM), BF16),
            pltpu.VMEM((2, n_chunks, c, HEAD_DIM), F32),
            pltpu.VMEM((2, n_chunks, SUBLANES, HEAD_DIM), F32),
        ],
        compiler_params=pltpu.CompilerParams(
            dimension_semantics=("arbitrary",), vmem_limit_bytes=VMEM_LIMIT),
        name="gdn",
    )(proj, proj, proj, conv_w, conv_w, conv_w, bg, proj, norm_w)


def _moba_kernel(q_ref, k_ref, v_ref, zs_ref, o_ref):
    s = q_ref.shape[0]
    blk = MOBA_BLOCK
    nb = s // blk

    prow = lax.broadcasted_iota(jnp.int32, (LANES, s), 0)
    pcol = lax.broadcasted_iota(jnp.int32, (LANES, s), 1)
    pind = jnp.where(prow == pcol // blk, 1.0 / blk, 0.0).astype(BF16)
    kmean = _dot(pind, k_ref[...])
    cand = 2 * SUBLANES
    km_full = kmean.astype(BF16)
    km_hi = km_full[0:cand, :]
    km_lo = (kmean - km_full.astype(F32)).astype(BF16)[0:cand, :]

    sub = lax.broadcasted_iota(jnp.int32, (cand, blk), 0)
    key_i = lax.broadcasted_iota(jnp.int32, (blk, blk), 0)
    qry_i = lax.broadcasted_iota(jnp.int32, (blk, blk), 1)
    causal = key_i <= qry_i
    v_t = [v_ref[j * blk:(j + 1) * blk, :].astype(F32).T.astype(BF16) for j in range(nb)]

    def score_stage(i):
        q = q_ref[i * blk:(i + 1) * blk, :]
        pieces = [_dot_nt(k_ref[j * blk:(j + 1) * blk, :], q) for j in range(i + 1)]
        gt = _dot_nt(km_hi, q) + _dot_nt(km_lo, q) if i > MOBA_TOPK else None
        return pieces, gt

    staged = score_stage(0)
    for i in range(nb):
        rows = slice(i * blk, (i + 1) * blk)
        pieces, gt = staged
        if i + 1 < nb:
            staged = score_stage(i + 1)
        if i > MOBA_TOPK:
            rank = jnp.zeros((cand, blk), jnp.int32)
            for jp in range(i):
                gj = jnp.broadcast_to(gt[jp:jp + 1, :], (cand, blk))
                ahead = jnp.logical_or(gj > gt, jnp.logical_and(gj == gt, sub > jp))
                rank = rank + ahead.astype(jnp.int32)
            bias = jnp.where(rank >= MOBA_TOPK, NEG_INF, 0.0)
            for j in range(i):
                pieces[j] = pieces[j] + bias[j:j + 1, :]
        pieces[i] = jnp.where(causal, pieces[i], NEG_INF)
        mt = pieces[0]
        for pc in pieces[1:]:
            mt = jnp.maximum(mt, pc)
        m = jnp.max(mt, axis=0, keepdims=True)
        pt = None
        o_t = jnp.zeros((HEAD_DIM, blk), F32)
        for j, pc in enumerate(pieces):
            p = jnp.exp2(pc - m)
            pt = p if pt is None else pt + p
            o_t = o_t + _dot(v_t[j], p.astype(BF16))
        l = jnp.sum(pt, axis=0, keepdims=True)
        o = (o_t / l).T
        o_ref[rows, :] = (o * zs_ref[rows, :].astype(F32)).astype(BF16)


def _moba(proj, batch, seq):
    blk = lambda off: pl.BlockSpec((seq, HEAD_DIM), lambda b, h: (b, off + h))
    return pl.pallas_call(
        _moba_kernel,
        grid=(batch, HEADS),
        in_specs=[blk(QKV_B), blk(QKV_B + HEADS), blk(QKV_B + 2 * HEADS), blk(Z_B)],
        out_specs=pl.BlockSpec((seq, HEAD_DIM), lambda b, h: (b, h)),
        out_shape=jax.ShapeDtypeStruct((batch * seq, HEADS * HEAD_DIM), BF16),
        compiler_params=pltpu.CompilerParams(
            dimension_semantics=("parallel", "parallel"), vmem_limit_bytes=VMEM_LIMIT),
        name="moba",
    )(proj, proj, proj, proj)


def _outproj_kernel(oa_ref, ob_ref, sa_ref, sb_ref, x_ref, wa_ref, wb_ref, wo_ref, pw_ref, o_ref):
    tm = o_ref.shape[0]
    halves = [slice(r, r + tm // OUT_SPLIT) for r in range(0, tm, tm // OUT_SPLIT)]
    ua = [_dot(oa_ref[r, :], wa_ref[...]) for r in halves]
    ub = [_dot(ob_ref[r, :], wb_ref[...]) for r in halves]
    merged = [(sa_ref[r, :].astype(F32) * a + sb_ref[r, :].astype(F32) * b).astype(BF16)
              for r, a, b in zip(halves, ua, ub)]
    ys = [_dot(mg, wo_ref[...]) for mg in merged]
    for r, y in zip(halves, ys):
        inv = lax.rsqrt(jnp.mean(y * y, axis=-1, keepdims=True) + NORM_EPS)
        o_ref[r, :] = x_ref[r, :] + y * inv * pw_ref[...]


def _outproj(oa, ob, proj, x2, wa, wb, wo, pw):
    m, d = x2.shape
    width = oa.shape[1]
    tm = OUT_TM
    const = lambda shape: pl.BlockSpec(shape, lambda i: (0, 0), pipeline_mode=pl.Buffered(1))
    return pl.pallas_call(
        _outproj_kernel,
        grid=(m // tm,),
        in_specs=[
            pl.BlockSpec((tm, width), lambda i: (i, 0)),
            pl.BlockSpec((tm, width), lambda i: (i, 0)),
            pl.BlockSpec((tm, d), lambda i: (i, GATE_A_BLK2048)),
            pl.BlockSpec((tm, d), lambda i: (i, GATE_B_BLK2048)),
            pl.BlockSpec((tm, d), lambda i: (i, 0)),
            const((width, d)), const((width, d)), const((d, d)), const((1, d)),
        ],
        out_specs=pl.BlockSpec((tm, d), lambda i: (i, 0)),
        out_shape=jax.ShapeDtypeStruct((m, d), F32),
        compiler_params=pltpu.CompilerParams(
            dimension_semantics=("parallel",), vmem_limit_bytes=VMEM_LIMIT),
        name="outproj",
    )(oa, ob, proj, proj, x2, wa, wb, wo, pw)


def _layer(x, pre_w, w_in, conv_w, a_log, dt_bias, gdn_norm_w, w_a, w_b, w_out, post_w):
    batch, seq, d = x.shape
    width = HEADS * HEAD_DIM
    ba_lo = 4 * width
    ba_hi = ba_lo + 2 * HEADS
    x2 = x.reshape(batch * seq, d)
    w_t = jnp.swapaxes(w_in, 0, 1).astype(BF16)
    w_ba = jnp.pad(w_t[ba_lo:ba_hi, :], ((0, LANES - 2 * HEADS), (0, 0)))
    pad_row = lambda v: jnp.pad(v.astype(F32), (HEADS, LANES - 2 * HEADS)).reshape(1, LANES)
    proj, bg = _inproj(x2, pre_w.reshape(1, d), w_t, w_ba, pad_row(a_log), pad_row(dt_bias))
    oa = _gdn(proj, conv_w, bg, gdn_norm_w.reshape(1, HEAD_DIM), batch, seq)
    ob = _moba(proj, batch, seq)
    out = _outproj(oa, ob, proj, x2, w_a.astype(BF16), w_b.astype(BF16), w_out.astype(BF16),
                   post_w.reshape(1, d))
    return out.reshape(batch, seq, d)


def kernel(x, pre_norm_w, w_in, conv_w, a_log, dt_bias, gdn_norm_w, w_branch_a, w_branch_b, w_out,
           post_norm_w):
    depth = pre_norm_w.shape[0]
    for l in range(depth):
        x = _layer(x, pre_norm_w[l], w_in[l], conv_w[l], a_log[l], dt_bias[l], gdn_norm_w[l],
                   w_branch_a[l], w_branch_b[l], w_out[l], post_norm_w[l])
    return x
```

```python
import functools

import jax
import jax.numpy as jnp
from jax import lax
from jax.experimental import pallas as pl
from jax.experimental.pallas import tpu as pltpu

F32 = jnp.float32
BF16 = jnp.bfloat16

NORM_EPS = 1e-6
NEG_INF = -1e30

HEADS = 8
HEAD_DIM = 128
CONV_WIDTH = 4
MOBA_BLOCK = 256
MOBA_TOPK = 3
GDN_CHUNK = 128

LANES = 128
SUBLANES = 8

QKV_A, Z_A, QKV_B, Z_B = 0, 24, 32, 56
GATE_A_BLK2048, GATE_B_BLK2048 = 4, 5

INPROJ_TM = 1024
INPROJ_TN = 1024
TILE_BLKS = INPROJ_TN // LANES
N_LO_TILES = (4 * HEADS * HEAD_DIM) // INPROJ_TN
OUT_TM = 512
OUT_SPLIT = 4

MOBA_QSCALE = (HEAD_DIM ** -0.5) * 1.4426950408889634

VMEM_LIMIT = 56 * 1024 * 1024


def _dot(a, b):
    return jnp.dot(a, b, preferred_element_type=F32)


def _dot_nt(a, b):
    return lax.dot_general(a, b, (((1,), (1,)), ((), ())), preferred_element_type=F32)


def _dot_tn(a, b):
    return lax.dot_general(a, b, (((0,), (0,)), ((), ())), preferred_element_type=F32)


def _sigmoid(x):
    return 0.5 * jnp.tanh(0.5 * x) + 0.5


def _softplus(x):
    return jnp.maximum(x, 0.0) + jnp.log1p(jnp.exp(-jnp.abs(x)))


def _inproj_kernel(x_ref, nw_ref, w_ref, wba_ref, alog_ref, dtb_ref, proj_ref, bg_ref, h_ref):
    j = pl.program_id(1)

    @pl.when(j == 0)
    def _():
        xf = x_ref[...]
        inv = lax.rsqrt(jnp.mean(xf * xf, axis=-1, keepdims=True) + NORM_EPS)
        h_ref[...] = (xf * inv * nw_ref[...]).astype(BF16)
        ba = _dot_nt(h_ref[...], wba_ref[...])
        lane = lax.broadcasted_iota(jnp.int32, ba.shape, 1)
        beta = _sigmoid(ba)
        g = -jnp.exp(alog_ref[...]) * _softplus(ba + dtb_ref[...])
        bg_ref[...] = jnp.where(lane < HEADS, beta, g)

    is_silu = jnp.logical_or(j == Z_A // TILE_BLKS, j == Z_B // TILE_BLKS)
    is_sig = j >= GATE_A_BLK2048 * 2
    lin = jnp.where(j == QKV_B // TILE_BLKS, MOBA_QSCALE, 1.0)

    acc = _dot_nt(h_ref[...], w_ref[...])

    @pl.when(jnp.logical_not(jnp.logical_or(is_silu, is_sig)))
    def _():
        proj_ref[...] = (acc * lin).astype(BF16)

    @pl.when(is_silu)
    def _():
        proj_ref[...] = (acc * _sigmoid(acc)).astype(BF16)

    @pl.when(is_sig)
    def _():
        proj_ref[...] = _sigmoid(acc).astype(BF16)


def _inproj(x2, nw, w_t, w_ba, alog_row, dtb_row):
    m, d = x2.shape
    tm, tn = INPROJ_TM, INPROJ_TN
    n = w_t.shape[0] - 2 * HEADS
    assert n % tn == 0 and N_LO_TILES * tn == 4 * HEADS * HEAD_DIM

    def w_rows(i, j):
        return (pl.multiple_of(j * tn + jnp.where(j >= N_LO_TILES, 2 * HEADS, 0), 2 * HEADS), 0)

    return pl.pallas_call(
        _inproj_kernel,
        grid=(m // tm, n // tn),
        in_specs=[
            pl.BlockSpec((tm, d), lambda i, j: (i, 0)),
            pl.BlockSpec((1, d), lambda i, j: (0, 0)),
            pl.BlockSpec((pl.Element(tn), pl.Element(d)), w_rows),
            pl.BlockSpec((LANES, d), lambda i, j: (0, 0)),
            pl.BlockSpec((1, LANES), lambda i, j: (0, 0)),
            pl.BlockSpec((1, LANES), lambda i, j: (0, 0)),
        ],
        out_specs=[
            pl.BlockSpec((tm, tn), lambda i, j: (i, j)),
            pl.BlockSpec((tm, LANES), lambda i, j: (i, 0)),
        ],
        out_shape=[
            jax.ShapeDtypeStruct((m, n), BF16),
            jax.ShapeDtypeStruct((m, LANES), F32),
        ],
        scratch_shapes=[pltpu.VMEM((tm, d), BF16)],
        compiler_params=pltpu.CompilerParams(
            dimension_semantics=("parallel", "arbitrary"), vmem_limit_bytes=VMEM_LIMIT),
        name="inproj",
    )(x2, nw, w_t, w_ba, alog_row, dtb_row)


def _interleave(*stages):
    live = list(stages)
    while live:
        for g in list(live):
            try:
                next(g)
            except StopIteration:
                live.remove(g)


def _stagger(*stages):
    live = []
    pending = list(stages)
    while live or pending:
        if pending:
            live.append(pending.pop(0))
        for g in list(live):
            try:
                next(g)
            except StopIteration:
                live.remove(g)


def _tri_inverse_group(lmats, row, col, out):
    c = lmats[0].shape[0]

    def sub_mask(s_log):
        bi = jnp.right_shift(row, s_log)
        bj = jnp.right_shift(col, s_log)
        return jnp.logical_and(jnp.bitwise_xor(bi, bj) == 1, jnp.bitwise_and(bi, 1) == 1)

    eye = (row == col).astype(F32)
    m0 = sub_mask(0)
    tds = [eye - jnp.where(m0, lm, 0.0) for lm in lmats]
    s_log = 1
    while (1 << s_log) < c:
        ms = sub_mask(s_log)
        tbs = [td.astype(BF16) for td in tds]
        x1s = [_dot(tb, jnp.where(ms, lm, 0.0).astype(BF16)).astype(BF16)
               for tb, lm in zip(tbs, lmats)]
        yield
        tds = [td - _dot(x1, tb) for td, x1, tb in zip(tds, x1s, tbs)]
        yield
        s_log += 1
    out.extend(tds)


def _gdn_kernel(q_ref, k_ref, v_ref, cwq_ref, cwk_ref, cwv_ref, bg_ref, zs_ref, nw_ref, o_ref,
                xpad, qn, kn, vn, betab, gb, xs, cs, qps, olocs, gls):
    s = q_ref.shape[0]
    c = GDN_CHUNK
    n_chunks = s // c
    t = pl.program_id(0)
    n_heads_total = pl.num_programs(0) - 1
    h = lax.rem(jnp.minimum(t, n_heads_total - 1), HEADS)
    cur = lax.rem(t, 2)
    prv = 1 - cur

    @pl.when(t == 0)
    def _():
        xs[1] = jnp.zeros(xs.shape[1:], xs.dtype)
        cs[1] = jnp.zeros(cs.shape[1:], cs.dtype)
        qps[1] = jnp.zeros(qps.shape[1:], qps.dtype)
        olocs[1] = jnp.zeros(olocs.shape[1:], olocs.dtype)
        gls[1] = jnp.zeros(gls.shape[1:], gls.dtype)

    def conv_silu(src_ref, cw_ref):
        xpad[0:SUBLANES, :] = jnp.zeros((SUBLANES, HEAD_DIM), F32)
        xpad[SUBLANES:SUBLANES + s, :] = src_ref[...].astype(F32)
        base = SUBLANES - (CONV_WIDTH - 1)
        acc = cw_ref[0:1, :] * xpad[base:base + s, :]
        for j in range(1, CONV_WIDTH):
            acc = acc + cw_ref[j:j + 1, :] * xpad[base + j:base + j + s, :]
        half = 0.5 * acc
        return half * (1.0 + jnp.tanh(half))

    def l2n(x, scale=1.0):
        return x * (lax.rsqrt(jnp.sum(x * x, axis=-1, keepdims=True) + NORM_EPS) * scale)

    qn[...] = l2n(conv_silu(q_ref, cwq_ref), HEAD_DIM ** -0.5)
    kn[...] = l2n(conv_silu(k_ref, cwk_ref))
    vn[...] = conv_silu(v_ref, cwv_ref)

    bg = bg_ref[...]
    lane = lax.broadcasted_iota(jnp.int32, bg.shape, 1)
    beta_col = jnp.sum(jnp.where(lane == h, bg, 0.0), axis=-1, keepdims=True)
    g_col = jnp.sum(jnp.where(lane == h + HEADS, bg, 0.0), axis=-1, keepdims=True)
    betab[...] = jnp.broadcast_to(beta_col, bg.shape)
    gb[...] = jnp.broadcast_to(g_col, bg.shape)

    row = lax.broadcasted_iota(jnp.int32, (c, c), 0)
    col = lax.broadcasted_iota(jnp.int32, (c, c), 1)
    incl = row >= col
    strict = row > col
    tril_b = incl.astype(BF16)

    def cumsum_rows(x):
        x1 = x.astype(BF16)
        r1 = x - x1.astype(F32)
        x2 = r1.astype(BF16)
        x3 = (r1 - x2.astype(F32)).astype(BF16)
        return _dot(tril_b, x1) + _dot(tril_b, x2) + _dot(tril_b, x3)

    def prep_stage(ns):
        rows = [slice(n * c, (n + 1) * c) for n in ns]
        lane = lax.broadcasted_iota(jnp.int32, (c, LANES), 1)
        gcol = jnp.zeros((c, LANES), F32)
        for n, r in zip(ns, rows):
            gcol = jnp.where(lane == n, gb[r, :], gcol)
        csum = cumsum_rows(gcol)
        csum_t = csum.T
        yield
        gcs = [jnp.broadcast_to(csum[:, n:n + 1], (c, c)) for n in ns]
        decays = [jnp.exp(jnp.where(incl, gc - jnp.broadcast_to(csum_t[n:n + 1, :], (c, c)),
                                    NEG_INF)) for n, gc in zip(ns, gcs)]
        a2s = [_dot_nt(jnp.concatenate([kn[r, :] * betab[r, :], qn[r, :]], axis=0).astype(BF16),
                       kn[r, :].astype(BF16)) for r in rows]
        yield
        lmats = [jnp.where(strict, a2[:c] * d, 0.0) for a2, d in zip(a2s, decays)]
        ts = []
        yield from _tri_inverse_group(lmats, row, col, ts)
        uwbs = []
        for r, gc, tinv in zip(rows, gcs, ts):
            beta = betab[r, :]
            rhs = jnp.concatenate([kn[r, :] * beta * jnp.exp(gc), vn[r, :] * beta], axis=1)
            uwbs.append(_dot(tinv.astype(BF16), rhs.astype(BF16)).astype(BF16))
        yield
        for n, r, gc, a2, d, uwb in zip(ns, rows, gcs, a2s, decays, uwbs):
            g_last = gc[c - 1:c, :]
            intra = jnp.where(incl, a2[c:] * d, 0.0)
            kdec = (kn[r, :] * jnp.exp(g_last - gc)).astype(BF16)
            kwu = _dot_tn(kdec, uwb)
            iwu = _dot(intra.astype(BF16), uwb)
            xs[cur, n] = kwu[:, :HEAD_DIM].astype(BF16)
            cs[cur, n] = kwu[:, HEAD_DIM:]
            qps[cur, n] = (qn[r, :] * jnp.exp(gc) - iwu[:, :HEAD_DIM]).astype(BF16)
            olocs[cur, n] = iwu[:, HEAD_DIM:]
            gls[cur, n] = jnp.broadcast_to(jnp.exp(g_last), (SUBLANES, HEAD_DIM))
            if n % 4 == 3:
                yield

    def recurrence_stage():
        nw = nw_ref[...]
        state = jnp.zeros((HEAD_DIM, HEAD_DIM), F32)
        for n in range(n_chunks):
            rows = slice(n * c, (n + 1) * c)
            sb = state.astype(BF16)
            o = _dot(qps[prv, n], sb) + olocs[prv, n]
            o = o * lax.rsqrt(jnp.mean(o * o, axis=-1, keepdims=True) + NORM_EPS)
            o = o * nw * zs_ref[rows, :].astype(F32)
            o_ref[rows, :] = o.astype(BF16)
            state = state * gls[prv, n][0:1, :] + cs[prv, n] - _dot(xs[prv, n], sb)
            yield

    _interleave(prep_stage(list(range(n_chunks))), recurrence_stage())


def _gdn(proj, conv_w, bg, norm_w, batch, seq):
    c = GDN_CHUNK
    n_chunks = seq // c
    n_heads_total = batch * HEADS
    rd = lambda t: jnp.minimum(t, n_heads_total - 1)
    wr = lambda t: jnp.maximum(t - 1, 0)
    blk = lambda off, hd: pl.BlockSpec(
        (seq, HEAD_DIM), lambda t: (hd(t) // HEADS, off + lax.rem(hd(t), HEADS)))
    cw = lambda off: pl.BlockSpec(
        (CONV_WIDTH, HEAD_DIM), lambda t: (0, off + lax.rem(rd(t), HEADS)))
    return pl.pallas_call(
        _gdn_kernel,
        grid=(n_heads_total + 1,),
        in_specs=[
            blk(QKV_A, rd), blk(QKV_A + HEADS, rd), blk(QKV_A + 2 * HEADS, rd),
            cw(0), cw(HEADS), cw(2 * HEADS),
            pl.BlockSpec((seq, LANES), lambda t: (rd(t) // HEADS, 0)),
            blk(Z_A, wr),
            pl.BlockSpec((1, HEAD_DIM), lambda t: (0, 0)),
        ],
        out_specs=blk(0, wr),
        out_shape=jax.ShapeDtypeStruct((batch * seq, HEADS * HEAD_DIM), BF16),
        scratch_shapes=[
            pltpu.VMEM((seq + SUBLANES, HEAD_DIM), F32),
            pltpu.VMEM((seq, HEAD_DIM), F32),
            pltpu.VMEM((seq, HEAD_DIM), F32),
            pltpu.VMEM((seq, HEAD_DIM), F32),
            pltpu.VMEM((seq, LANES), F32),
            pltpu.VMEM((seq, LANES), F32),
            pltpu.VMEM((2, n_chunks, HEAD_DIM, HEAD_DIM), BF16),
            pltpu.VMEM((2, n_chunks, HEAD_DIM, HEAD_DIM), F32),
            pltpu.VMEM((2, n_chunks, c, HEAD_DIM), BF16),
            pltpu.VMEM((2, n_chunks, c, HEAD_DIM), F32),
            pltpu.VMEM((2, n_chunks, SUBLANES, HEAD_DIM), F32),
        ],
        compiler_params=pltpu.CompilerParams(
            dimension_semantics=("arbitrary",), vmem_limit_bytes=VMEM_LIMIT),
        name="gdn",
    )(proj, proj, proj, conv_w, conv_w, conv_w, bg, proj, norm_w)


def _moba_kernel(q_ref, k_ref, v_ref, zs_ref, o_ref):
    s = q_ref.shape[0]
    blk = MOBA_BLOCK
    nb = s // blk

    prow = lax.broadcasted_iota(jnp.int32, (LANES, s), 0)
    pcol = lax.broadcasted_iota(jnp.int32, (LANES, s), 1)
    pind = jnp.where(prow == pcol // blk, 1.0 / blk, 0.0).astype(BF16)
    kmean = _dot(pind, k_ref[...])
    cand = 2 * SUBLANES
    km_full = kmean.astype(BF16)
    km_hi = km_full[0:cand, :]
    km_lo = (kmean - km_full.astype(F32)).astype(BF16)[0:cand, :]

    sub = lax.broadcasted_iota(jnp.int32, (cand, blk), 0)
    key_i = lax.broadcasted_iota(jnp.int32, (blk, blk), 0)
    qry_i = lax.broadcasted_iota(jnp.int32, (blk, blk), 1)
    causal = key_i <= qry_i
    v_t = [v_ref[j * blk:(j + 1) * blk, :].astype(F32).T.astype(BF16) for j in range(nb)]

    def query_block(i):
        rows = slice(i * blk, (i + 1) * blk)
        q = q_ref[rows, :]
        pieces = [_dot_nt(k_ref[j * blk:(j + 1) * blk, :], q) for j in range(i + 1)]
        gt = _dot_nt(km_hi, q) + _dot_nt(km_lo, q) if i > MOBA_TOPK else None
        yield
        if i > MOBA_TOPK:
            rank = jnp.zeros((cand, blk), jnp.int32)
            for jp in range(i):
                gj = jnp.broadcast_to(gt[jp:jp + 1, :], (cand, blk))
                ahead = jnp.logical_or(gj > gt, jnp.logical_and(gj == gt, sub > jp))
                rank = rank + ahead.astype(jnp.int32)
            bias = jnp.where(rank >= MOBA_TOPK, NEG_INF, 0.0)
            for j in range(i):
                pieces[j] = pieces[j] + bias[j:j + 1, :]
        pieces[i] = jnp.where(causal, pieces[i], NEG_INF)
        mt = pieces[0]
        for pc in pieces[1:]:
            mt = jnp.maximum(mt, pc)
        m = jnp.max(mt, axis=0, keepdims=True)
        yield
        pt = None
        o_t = jnp.zeros((HEAD_DIM, blk), F32)
        for j, pc in enumerate(pieces):
            p = jnp.exp2(pc - m)
            pt = p if pt is None else pt + p
            o_t = o_t + _dot(v_t[j], p.astype(BF16))
            if j % 2 == 1:
                yield
        yield
        l = jnp.sum(pt, axis=0, keepdims=True)
        o = (o_t / l).T
        o_ref[rows, :] = (o * zs_ref[rows, :].astype(F32)).astype(BF16)

    order = [b for i in range(nb // 2) for b in (nb - 1 - i, i)]
    _stagger(*[query_block(i) for i in order])


def _moba(proj, batch, seq):
    blk = lambda off: pl.BlockSpec((seq, HEAD_DIM), lambda b, h: (b, off + h))
    return pl.pallas_call(
        _moba_kernel,
        grid=(batch, HEADS),
        in_specs=[blk(QKV_B), blk(QKV_B + HEADS), blk(QKV_B + 2 * HEADS), blk(Z_B)],
        out_specs=pl.BlockSpec((seq, HEAD_DIM), lambda b, h: (b, h)),
        out_shape=jax.ShapeDtypeStruct((batch * seq, HEADS * HEAD_DIM), BF16),
        compiler_params=pltpu.CompilerParams(
            dimension_semantics=("parallel", "parallel"), vmem_limit_bytes=VMEM_LIMIT),
        name="moba",
    )(proj, proj, proj, proj)


def _outproj_kernel(oa_ref, ob_ref, sa_ref, sb_ref, x_ref, wa_ref, wb_ref, wo_ref, pw_ref, o_ref):
    tm = o_ref.shape[0]
    halves = [slice(r, r + tm // OUT_SPLIT) for r in range(0, tm, tm // OUT_SPLIT)]
    ua = [_dot(oa_ref[r, :], wa_ref[...]) for r in halves]
    ub = [_dot(ob_ref[r, :], wb_ref[...]) for r in halves]
    merged = [(sa_ref[r, :].astype(F32) * a + sb_ref[r, :].astype(F32) * b).astype(BF16)
              for r, a, b in zip(halves, ua, ub)]
    ys = [_dot(mg, wo_ref[...]) for mg in merged]
    for r, y in zip(halves, ys):
        inv = lax.rsqrt(jnp.mean(y * y, axis=-1, keepdims=True) + NORM_EPS)
        o_ref[r, :] = x_ref[r, :] + y * inv * pw_ref[...]


def _outproj(oa, ob, proj, x2, wa, wb, wo, pw):
    m, d = x2.shape
    width = oa.shape[1]
    tm = OUT_TM
    const = lambda shape: pl.BlockSpec(shape, lambda i: (0, 0), pipeline_mode=pl.Buffered(1))
    return pl.pallas_call(
        _outproj_kernel,
        grid=(m // tm,),
        in_specs=[
            pl.BlockSpec((tm, width), lambda i: (i, 0)),
            pl.BlockSpec((tm, width), lambda i: (i, 0)),
            pl.BlockSpec((tm, d), lambda i: (i, GATE_A_BLK2048)),
            pl.BlockSpec((tm, d), lambda i: (i, GATE_B_BLK2048)),
            pl.BlockSpec((tm, d), lambda i: (i, 0)),
            const((width, d)), const((width, d)), const((d, d)), const((1, d)),
        ],
        out_specs=pl.BlockSpec((tm, d), lambda i: (i, 0)),
        out_shape=jax.ShapeDtypeStruct((m, d), F32),
        compiler_params=pltpu.CompilerParams(
            dimension_semantics=("parallel",), vmem_limit_bytes=VMEM_LIMIT),
        name="outproj",
    )(oa, ob, proj, proj, x2, wa, wb, wo, pw)


def _layer(x, pre_w, w_in, conv_w, a_log, dt_bias, gdn_norm_w, w_a, w_b, w_out, post_w):
    batch, seq, d = x.shape
    width = HEADS * HEAD_DIM
    ba_lo = 4 * width
    ba_hi = ba_lo + 2 * HEADS
    x2 = x.reshape(batch * seq, d)
    w_t = jnp.swapaxes(w_in, 0, 1).astype(BF16)
    w_ba = jnp.pad(w_t[ba_lo:ba_hi, :], ((0, LANES - 2 * HEADS), (0, 0)))
    pad_row = lambda v: jnp.pad(v.astype(F32), (HEADS, LANES - 2 * HEADS)).reshape(1, LANES)
    proj, bg = _inproj(x2, pre_w.reshape(1, d), w_t, w_ba, pad_row(a_log), pad_row(dt_bias))
    oa = _gdn(proj, conv_w, bg, gdn_norm_w.reshape(1, HEAD_DIM), batch, seq)
    ob = _moba(proj, batch, seq)
    out = _outproj(oa, ob, proj, x2, w_a.astype(BF16), w_b.astype(BF16), w_out.astype(BF16),
                   post_w.reshape(1, d))
    return out.reshape(batch, seq, d)


def kernel(x, pre_norm_w, w_in, conv_w, a_log, dt_bias, gdn_norm_w, w_branch_a, w_branch_b, w_out,
           post_norm_w):
    depth = pre_norm_w.shape[0]
    for l in range(depth):
        x = _layer(x, pre_norm_w[l], w_in[l], conv_w[l], a_log[l], dt_bias[l], gdn_norm_w[l],
                   w_branch_a[l], w_branch_b[l], w_out[l], post_norm_w[l])
    return x
```

```python
import functools

import jax
import jax.numpy as jnp
from jax import lax
from jax.experimental import pallas as pl
from jax.experimental.pallas import tpu as pltpu

F32 = jnp.float32
BF16 = jnp.bfloat16

NORM_EPS = 1e-6
NEG_INF = -1e30

HEADS = 8
HEAD_DIM = 128
CONV_WIDTH = 4
MOBA_BLOCK = 256
MOBA_TOPK = 3
GDN_CHUNK = 128
GDN_GROUP = 16

LANES = 128
SUBLANES = 8

QKV_A, Z_A, QKV_B, Z_B = 0, 24, 32, 56
GATE_A_BLK2048, GATE_B_BLK2048 = 4, 5

INPROJ_TM = 1024
INPROJ_TN = 1024
TILE_BLKS = INPROJ_TN // LANES
N_LO_TILES = (4 * HEADS * HEAD_DIM) // INPROJ_TN
OUT_TM = 512
OUT_SPLIT = 4

MOBA_QSCALE = (HEAD_DIM ** -0.5) * 1.4426950408889634

VMEM_LIMIT = 56 * 1024 * 1024


def _dot(a, b):
    return jnp.dot(a, b, preferred_element_type=F32)


def _dot_nt(a, b):
    return lax.dot_general(a, b, (((1,), (1,)), ((), ())), preferred_element_type=F32)


def _dot_tn(a, b):
    return lax.dot_general(a, b, (((0,), (0,)), ((), ())), preferred_element_type=F32)


def _sigmoid(x):
    return 0.5 * jnp.tanh(0.5 * x) + 0.5


def _softplus(x):
    return jnp.maximum(x, 0.0) + jnp.log1p(jnp.exp(-jnp.abs(x)))


def _inproj_kernel(x_ref, nw_ref, w_ref, wba_ref, alog_ref, dtb_ref, proj_ref, bg_ref, h_ref):
    j = pl.program_id(1)

    @pl.when(j == 0)
    def _():
        xf = x_ref[...]
        inv = lax.rsqrt(jnp.mean(xf * xf, axis=-1, keepdims=True) + NORM_EPS)
        h_ref[...] = (xf * inv * nw_ref[...]).astype(BF16)
        ba = _dot_nt(h_ref[...], wba_ref[...])
        lane = lax.broadcasted_iota(jnp.int32, ba.shape, 1)
        beta = _sigmoid(ba)
        g = -jnp.exp(alog_ref[...]) * _softplus(ba + dtb_ref[...])
        bg_ref[...] = jnp.where(lane < HEADS, beta, g)

    is_silu = jnp.logical_or(j == Z_A // TILE_BLKS, j == Z_B // TILE_BLKS)
    is_sig = j >= GATE_A_BLK2048 * 2
    lin = jnp.where(j == QKV_B // TILE_BLKS, MOBA_QSCALE, 1.0)

    acc = _dot_nt(h_ref[...], w_ref[...])

    @pl.when(jnp.logical_not(jnp.logical_or(is_silu, is_sig)))
    def _():
        proj_ref[...] = (acc * lin).astype(BF16)

    @pl.when(is_silu)
    def _():
        proj_ref[...] = (acc * _sigmoid(acc)).astype(BF16)

    @pl.when(is_sig)
    def _():
        proj_ref[...] = _sigmoid(acc).astype(BF16)


def _inproj(x2, nw, w_t, w_ba, alog_row, dtb_row):
    m, d = x2.shape
    tm, tn = INPROJ_TM, INPROJ_TN
    n = w_t.shape[0] - 2 * HEADS
    assert n % tn == 0 and N_LO_TILES * tn == 4 * HEADS * HEAD_DIM

    def w_rows(i, j):
        return (pl.multiple_of(j * tn + jnp.where(j >= N_LO_TILES, 2 * HEADS, 0), 2 * HEADS), 0)

    return pl.pallas_call(
        _inproj_kernel,
        grid=(m // tm, n // tn),
        in_specs=[
            pl.BlockSpec((tm, d), lambda i, j: (i, 0)),
            pl.BlockSpec((1, d), lambda i, j: (0, 0)),
            pl.BlockSpec((pl.Element(tn), pl.Element(d)), w_rows),
            pl.BlockSpec((LANES, d), lambda i, j: (0, 0)),
            pl.BlockSpec((1, LANES), lambda i, j: (0, 0)),
            pl.BlockSpec((1, LANES), lambda i, j: (0, 0)),
        ],
        out_specs=[
            pl.BlockSpec((tm, tn), lambda i, j: (i, j)),
            pl.BlockSpec((tm, LANES), lambda i, j: (i, 0)),
        ],
        out_shape=[
            jax.ShapeDtypeStruct((m, n), BF16),
            jax.ShapeDtypeStruct((m, LANES), F32),
        ],
        scratch_shapes=[pltpu.VMEM((tm, d), BF16)],
        compiler_params=pltpu.CompilerParams(
            dimension_semantics=("parallel", "arbitrary"), vmem_limit_bytes=VMEM_LIMIT),
        name="inproj",
    )(x2, nw, w_t, w_ba, alog_row, dtb_row)


def _interleave(*stages):
    live = list(stages)
    while live:
        for g in list(live):
            try:
                next(g)
            except StopIteration:
                live.remove(g)


def _stagger(*stages):
    live = []
    pending = list(stages)
    while live or pending:
        if pending:
            live.append(pending.pop(0))
        for g in list(live):
            try:
                next(g)
            except StopIteration:
                live.remove(g)
        yield


def _tri_inverse_group(lmats, row, col, out):
    c = lmats[0].shape[0]

    def sub_mask(s_log):
        bi = jnp.right_shift(row, s_log)
        bj = jnp.right_shift(col, s_log)
        return jnp.logical_and(jnp.bitwise_xor(bi, bj) == 1, jnp.bitwise_and(bi, 1) == 1)

    eye = (row == col).astype(F32)
    m0 = sub_mask(0)
    tds = [eye - jnp.where(m0, lm, 0.0) for lm in lmats]
    s_log = 1
    while (1 << s_log) < c:
        ms = sub_mask(s_log)
        tbs = [td.astype(BF16) for td in tds]
        x1s = [_dot(tb, jnp.where(ms, lm, 0.0).astype(BF16)).astype(BF16)
               for tb, lm in zip(tbs, lmats)]
        yield
        tds = [td - _dot(x1, tb) for td, x1, tb in zip(tds, x1s, tbs)]
        yield
        s_log += 1
    out.extend(tds)


def _mixers_kernel(q_ref, k_ref, v_ref, cwq_ref, cwk_ref, cwv_ref, bg_ref, zs_ref, nw_ref,
                   mq_ref, mk_ref, mv_ref, mzs_ref, o_ref, mo_ref,
                   xpad, qn, kn, vn, betab, gb, xs, cs, qps, olocs, gls):
    t = pl.program_id(0)

    @pl.when(t == 0)
    def _():
        xs[1] = jnp.zeros(xs.shape[1:], xs.dtype)
        cs[1] = jnp.zeros(cs.shape[1:], cs.dtype)
        qps[1] = jnp.zeros(qps.shape[1:], qps.dtype)
        olocs[1] = jnp.zeros(olocs.shape[1:], olocs.dtype)
        gls[1] = jnp.zeros(gls.shape[1:], gls.dtype)

    gdn_stages = _gdn_stages(q_ref, k_ref, v_ref, cwq_ref, cwk_ref, cwv_ref, bg_ref, zs_ref, nw_ref,
                             o_ref, xpad, qn, kn, vn, betab, gb, xs, cs, qps, olocs, gls)
    moba_stage = _moba_stages(mq_ref, mk_ref, mv_ref, mzs_ref, mo_ref)
    _interleave(*gdn_stages, moba_stage)


def _gdn_stages(q_ref, k_ref, v_ref, cwq_ref, cwk_ref, cwv_ref, bg_ref, zs_ref, nw_ref, o_ref,
                xpad, qn, kn, vn, betab, gb, xs, cs, qps, olocs, gls):
    s = q_ref.shape[0]
    c = GDN_CHUNK
    n_chunks = s // c
    t = pl.program_id(0)
    n_heads_total = pl.num_programs(0) - 1
    h = lax.rem(jnp.minimum(t, n_heads_total - 1), HEADS)
    cur = lax.rem(t, 2)
    prv = 1 - cur

    def conv_silu(src_ref, cw_ref):
        xpad[0:SUBLANES, :] = jnp.zeros((SUBLANES, HEAD_DIM), F32)
        xpad[SUBLANES:SUBLANES + s, :] = src_ref[...].astype(F32)
        base = SUBLANES - (CONV_WIDTH - 1)
        acc = cw_ref[0:1, :] * xpad[base:base + s, :]
        for j in range(1, CONV_WIDTH):
            acc = acc + cw_ref[j:j + 1, :] * xpad[base + j:base + j + s, :]
        half = 0.5 * acc
        return half * (1.0 + jnp.tanh(half))

    def l2n(x, scale=1.0):
        return x * (lax.rsqrt(jnp.sum(x * x, axis=-1, keepdims=True) + NORM_EPS) * scale)

    qn[...] = l2n(conv_silu(q_ref, cwq_ref), HEAD_DIM ** -0.5)
    kn[...] = l2n(conv_silu(k_ref, cwk_ref))
    vn[...] = conv_silu(v_ref, cwv_ref)

    bg = bg_ref[...]
    lane = lax.broadcasted_iota(jnp.int32, bg.shape, 1)
    beta_col = jnp.sum(jnp.where(lane == h, bg, 0.0), axis=-1, keepdims=True)
    g_col = jnp.sum(jnp.where(lane == h + HEADS, bg, 0.0), axis=-1, keepdims=True)
    betab[...] = jnp.broadcast_to(beta_col, bg.shape)
    gb[...] = jnp.broadcast_to(g_col, bg.shape)

    row = lax.broadcasted_iota(jnp.int32, (c, c), 0)
    col = lax.broadcasted_iota(jnp.int32, (c, c), 1)
    incl = row >= col
    strict = row > col
    tril_b = incl.astype(BF16)

    def cumsum_rows(x):
        x1 = x.astype(BF16)
        r1 = x - x1.astype(F32)
        x2 = r1.astype(BF16)
        x3 = (r1 - x2.astype(F32)).astype(BF16)
        return _dot(tril_b, x1) + _dot(tril_b, x2) + _dot(tril_b, x3)

    def prep_stage(ns):
        rows = [slice(n * c, (n + 1) * c) for n in ns]
        lane = lax.broadcasted_iota(jnp.int32, (c, LANES), 1)
        gcol = jnp.zeros((c, LANES), F32)
        for n, r in zip(ns, rows):
            gcol = jnp.where(lane == n, gb[r, :], gcol)
        csum = cumsum_rows(gcol)
        csum_t = csum.T
        yield
        gcs = [jnp.broadcast_to(csum[:, n:n + 1], (c, c)) for n in ns]
        decays = [jnp.exp(jnp.where(incl, gc - jnp.broadcast_to(csum_t[n:n + 1, :], (c, c)),
                                    NEG_INF)) for n, gc in zip(ns, gcs)]
        a2s = [_dot_nt(jnp.concatenate([kn[r, :] * betab[r, :], qn[r, :]], axis=0).astype(BF16),
                       kn[r, :].astype(BF16)) for r in rows]
        yield
        lmats = [jnp.where(strict, a2[:c] * d, 0.0) for a2, d in zip(a2s, decays)]
        ts = []
        yield from _tri_inverse_group(lmats, row, col, ts)
        uwbs = []
        for r, gc, tinv in zip(rows, gcs, ts):
            beta = betab[r, :]
            rhs = jnp.concatenate([kn[r, :] * beta * jnp.exp(gc), vn[r, :] * beta], axis=1)
            uwbs.append(_dot(tinv.astype(BF16), rhs.astype(BF16)).astype(BF16))
        yield
        for n, r, gc, a2, d, uwb in zip(ns, rows, gcs, a2s, decays, uwbs):
            g_last = gc[c - 1:c, :]
            intra = jnp.where(incl, a2[c:] * d, 0.0)
            kdec = (kn[r, :] * jnp.exp(g_last - gc)).astype(BF16)
            kwu = _dot_tn(kdec, uwb)
            iwu = _dot(intra.astype(BF16), uwb)
            xs[cur, n] = kwu[:, :HEAD_DIM].astype(BF16)
            cs[cur, n] = kwu[:, HEAD_DIM:]
            qps[cur, n] = (qn[r, :] * jnp.exp(gc) - iwu[:, :HEAD_DIM]).astype(BF16)
            olocs[cur, n] = iwu[:, HEAD_DIM:]
            gls[cur, n] = jnp.broadcast_to(jnp.exp(g_last), (SUBLANES, HEAD_DIM))
            if n % 4 == 3:
                yield

    def recurrence_stage():
        nw = nw_ref[...]
        state = jnp.zeros((HEAD_DIM, HEAD_DIM), F32)
        for n in range(n_chunks):
            rows = slice(n * c, (n + 1) * c)
            sb = state.astype(BF16)
            o = _dot(qps[prv, n], sb) + olocs[prv, n]
            o = o * lax.rsqrt(jnp.mean(o * o, axis=-1, keepdims=True) + NORM_EPS)
            o = o * nw * zs_ref[rows, :].astype(F32)
            o_ref[rows, :] = o.astype(BF16)
            state = state * gls[prv, n][0:1, :] + cs[prv, n] - _dot(xs[prv, n], sb)
            yield

    def prep_groups():
        for g0 in range(0, n_chunks, GDN_GROUP):
            yield from prep_stage(list(range(g0, g0 + GDN_GROUP)))

    return prep_groups(), recurrence_stage()


def _mixers(proj, conv_w, bg, norm_w, batch, seq):
    c = GDN_CHUNK
    n_chunks = seq // c
    n_heads_total = batch * HEADS
    rd = lambda t: jnp.minimum(t, n_heads_total - 1)
    wr = lambda t: jnp.maximum(t - 1, 0)
    blk = lambda off, hd: pl.BlockSpec(
        (seq, HEAD_DIM), lambda t: (hd(t) // HEADS, off + lax.rem(hd(t), HEADS)))
    cw = lambda off: pl.BlockSpec(
        (CONV_WIDTH, HEAD_DIM), lambda t: (0, off + lax.rem(rd(t), HEADS)))
    out = jax.ShapeDtypeStruct((batch * seq, HEADS * HEAD_DIM), BF16)
    return pl.pallas_call(
        _mixers_kernel,
        grid=(n_heads_total + 1,),
        in_specs=[
            blk(QKV_A, rd), blk(QKV_A + HEADS, rd), blk(QKV_A + 2 * HEADS, rd),
            cw(0), cw(HEADS), cw(2 * HEADS),
            pl.BlockSpec((seq, LANES), lambda t: (rd(t) // HEADS, 0)),
            blk(Z_A, wr),
            pl.BlockSpec((1, HEAD_DIM), lambda t: (0, 0)),
            blk(QKV_B, rd), blk(QKV_B + HEADS, rd), blk(QKV_B + 2 * HEADS, rd), blk(Z_B, rd),
        ],
        out_specs=[blk(0, wr), blk(0, rd)],
        out_shape=[out, out],
        scratch_shapes=[
            pltpu.VMEM((seq + SUBLANES, HEAD_DIM), F32),
            pltpu.VMEM((seq, HEAD_DIM), F32),
            pltpu.VMEM((seq, HEAD_DIM), F32),
            pltpu.VMEM((seq, HEAD_DIM), F32),
            pltpu.VMEM((seq, LANES), F32),
            pltpu.VMEM((seq, LANES), F32),
            pltpu.VMEM((2, n_chunks, HEAD_DIM, HEAD_DIM), BF16),
            pltpu.VMEM((2, n_chunks, HEAD_DIM, HEAD_DIM), F32),
            pltpu.VMEM((2, n_chunks, c, HEAD_DIM), BF16),
            pltpu.VMEM((2, n_chunks, c, HEAD_DIM), F32),
            pltpu.VMEM((2, n_chunks, SUBLANES, HEAD_DIM), F32),
        ],
        compiler_params=pltpu.CompilerParams(
            dimension_semantics=("arbitrary",), vmem_limit_bytes=VMEM_LIMIT),
        name="mixers",
    )(proj, proj, proj, conv_w, conv_w, conv_w, bg, proj, norm_w, proj, proj, proj, proj)


def _moba_stages(q_ref, k_ref, v_ref, zs_ref, o_ref):
    s = q_ref.shape[0]
    blk = MOBA_BLOCK
    nb = s // blk

    prow = lax.broadcasted_iota(jnp.int32, (LANES, s), 0)
    pcol = lax.broadcasted_iota(jnp.int32, (LANES, s), 1)
    pind = jnp.where(prow == pcol // blk, 1.0 / blk, 0.0).astype(BF16)
    kmean = _dot(pind, k_ref[...])
    cand = 2 * SUBLANES
    km_full = kmean.astype(BF16)
    km_hi = km_full[0:cand, :]
    km_lo = (kmean - km_full.astype(F32)).astype(BF16)[0:cand, :]

    sub = lax.broadcasted_iota(jnp.int32, (cand, blk), 0)
    key_i = lax.broadcasted_iota(jnp.int32, (blk, blk), 0)
    qry_i = lax.broadcasted_iota(jnp.int32, (blk, blk), 1)
    causal = key_i <= qry_i
    v_t = [v_ref[j * blk:(j + 1) * blk, :].astype(F32).T.astype(BF16) for j in range(nb)]

    def query_block(i):
        rows = slice(i * blk, (i + 1) * blk)
        q = q_ref[rows, :]
        pieces = [_dot_nt(k_ref[j * blk:(j + 1) * blk, :], q) for j in range(i + 1)]
        gt = _dot_nt(km_hi, q) + _dot_nt(km_lo, q) if i > MOBA_TOPK else None
        yield
        if i > MOBA_TOPK:
            rank = jnp.zeros((cand, blk), jnp.int32)
            for jp in range(i):
                gj = jnp.broadcast_to(gt[jp:jp + 1, :], (cand, blk))
                ahead = jnp.logical_or(gj > gt, jnp.logical_and(gj == gt, sub > jp))
                rank = rank + ahead.astype(jnp.int32)
            bias = jnp.where(rank >= MOBA_TOPK, NEG_INF, 0.0)
            for j in range(i):
                pieces[j] = pieces[j] + bias[j:j + 1, :]
        pieces[i] = jnp.where(causal, pieces[i], NEG_INF)
        mt = pieces[0]
        for pc in pieces[1:]:
            mt = jnp.maximum(mt, pc)
        m = jnp.max(mt, axis=0, keepdims=True)
        yield
        pt = None
        o_t = jnp.zeros((HEAD_DIM, blk), F32)
        for j, pc in enumerate(pieces):
            p = jnp.exp2(pc - m)
            pt = p if pt is None else pt + p
            o_t = o_t + _dot(v_t[j], p.astype(BF16))
            if j % 2 == 1:
                yield
        yield
        l = jnp.sum(pt, axis=0, keepdims=True)
        o = (o_t / l).T
        o_ref[rows, :] = (o * zs_ref[rows, :].astype(F32)).astype(BF16)

    order = [b for i in range(nb // 2) for b in (nb - 1 - i, i)]
    return _stagger(*[query_block(i) for i in order])


def _outproj_kernel(oa_ref, ob_ref, sa_ref, sb_ref, x_ref, wa_ref, wb_ref, wo_ref, pw_ref, o_ref):
    tm = o_ref.shape[0]
    halves = [slice(r, r + tm // OUT_SPLIT) for r in range(0, tm, tm // OUT_SPLIT)]
    ua = [_dot(oa_ref[r, :], wa_ref[...]) for r in halves]
    ub = [_dot(ob_ref[r, :], wb_ref[...]) for r in halves]
    merged = [(sa_ref[r, :].astype(F32) * a + sb_ref[r, :].astype(F32) * b).astype(BF16)
              for r, a, b in zip(halves, ua, ub)]
    ys = [_dot(mg, wo_ref[...]) for mg in merged]
    for r, y in zip(halves, ys):
        inv = lax.rsqrt(jnp.mean(y * y, axis=-1, keepdims=True) + NORM_EPS)
        o_ref[r, :] = x_ref[r, :] + y * inv * pw_ref[...]


def _outproj(oa, ob, proj, x2, wa, wb, wo, pw):
    m, d = x2.shape
    width = oa.shape[1]
    tm = OUT_TM
    const = lambda shape: pl.BlockSpec(shape, lambda i: (0, 0), pipeline_mode=pl.Buffered(1))
    return pl.pallas_call(
        _outproj_kernel,
        grid=(m // tm,),
        in_specs=[
            pl.BlockSpec((tm, width), lambda i: (i, 0)),
            pl.BlockSpec((tm, width), lambda i: (i, 0)),
            pl.BlockSpec((tm, d), lambda i: (i, GATE_A_BLK2048)),
            pl.BlockSpec((tm, d), lambda i: (i, GATE_B_BLK2048)),
            pl.BlockSpec((tm, d), lambda i: (i, 0)),
            const((width, d)), const((width, d)), const((d, d)), const((1, d)),
        ],
        out_specs=pl.BlockSpec((tm, d), lambda i: (i, 0)),
        out_shape=jax.ShapeDtypeStruct((m, d), F32),
        compiler_params=pltpu.CompilerParams(
            dimension_semantics=("parallel",), vmem_limit_bytes=VMEM_LIMIT),
        name="outproj",
    )(oa, ob, proj, proj, x2, wa, wb, wo, pw)


def _layer(x, pre_w, w_in, conv_w, a_log, dt_bias, gdn_norm_w, w_a, w_b, w_out, post_w):
    batch, seq, d = x.shape
    width = HEADS * HEAD_DIM
    ba_lo = 4 * width
    ba_hi = ba_lo + 2 * HEADS
    x2 = x.reshape(batch * seq, d)
    w_t = jnp.swapaxes(w_in, 0, 1).astype(BF16)
    w_ba = jnp.pad(w_t[ba_lo:ba_hi, :], ((0, LANES - 2 * HEADS), (0, 0)))
    pad_row = lambda v: jnp.pad(v.astype(F32), (HEADS, LANES - 2 * HEADS)).reshape(1, LANES)
    proj, bg = _inproj(x2, pre_w.reshape(1, d), w_t, w_ba, pad_row(a_log), pad_row(dt_bias))
    oa, ob = _mixers(proj, conv_w, bg, gdn_norm_w.reshape(1, HEAD_DIM), batch, seq)
    out = _outproj(oa, ob, proj, x2, w_a.astype(BF16), w_b.astype(BF16), w_out.astype(BF16),
                   post_w.reshape(1, d))
    return out.reshape(batch, seq, d)


def kernel(x, pre_norm_w, w_in, conv_w, a_log, dt_bias, gdn_norm_w, w_branch_a, w_branch_b, w_out,
           post_norm_w):
    depth = pre_norm_w.shape[0]
    for l in range(depth):
        x = _layer(x, pre_norm_w[l], w_in[l], conv_w[l], a_log[l], dt_bias[l], gdn_norm_w[l],
                   w_branch_a[l], w_branch_b[l], w_out[l], post_norm_w[l])
    return x
```

```python
import functools

import jax
import jax.numpy as jnp
from jax import lax
from jax.experimental import pallas as pl
from jax.experimental.pallas import tpu as pltpu

F32 = jnp.float32
BF16 = jnp.bfloat16

NORM_EPS = 1e-6
NEG_INF = -1e30

HEADS = 8
HEAD_DIM = 128
CONV_WIDTH = 4
MOBA_BLOCK = 256
MOBA_TOPK = 3
GDN_CHUNK = 128
GDN_GROUP = 16

LANES = 128
SUBLANES = 8

QKV_A, Z_A, QKV_B, Z_B = 0, 24, 32, 56
GATE_A_BLK2048, GATE_B_BLK2048 = 4, 5

INPROJ_TM = 1024
INPROJ_TN = 1024
TILE_BLKS = INPROJ_TN // LANES
N_LO_TILES = (4 * HEADS * HEAD_DIM) // INPROJ_TN
OUT_TM = 512
OUT_SPLIT = 4

MOBA_QSCALE = (HEAD_DIM ** -0.5) * 1.4426950408889634

VMEM_LIMIT = 56 * 1024 * 1024


def _dot(a, b):
    return jnp.dot(a, b, preferred_element_type=F32)


def _dot_nt(a, b):
    return lax.dot_general(a, b, (((1,), (1,)), ((), ())), preferred_element_type=F32)


def _dot_tn(a, b):
    return lax.dot_general(a, b, (((0,), (0,)), ((), ())), preferred_element_type=F32)


def _sigmoid(x):
    return 0.5 * jnp.tanh(0.5 * x) + 0.5


def _softplus(x):
    return jnp.maximum(x, 0.0) + jnp.log1p(jnp.exp(-jnp.abs(x)))


def _inproj_kernel(x_ref, nw_ref, w_ref, wba_ref, alog_ref, dtb_ref, proj_ref, bg_ref, h_ref):
    j = pl.program_id(1)

    @pl.when(j == 0)
    def _():
        xf = x_ref[...]
        inv = lax.rsqrt(jnp.mean(xf * xf, axis=-1, keepdims=True) + NORM_EPS)
        h_ref[...] = (xf * inv * nw_ref[...]).astype(h_ref.dtype)
        ba = _dot_nt(h_ref[...], wba_ref[...])
        lane = lax.broadcasted_iota(jnp.int32, ba.shape, 1)
        beta = _sigmoid(ba)
        g = -jnp.exp(alog_ref[...]) * _softplus(ba + dtb_ref[...])
        bg_ref[...] = jnp.where(lane < HEADS, beta, g)

    is_silu = jnp.logical_or(j == Z_A // TILE_BLKS, j == Z_B // TILE_BLKS)
    is_sig = j >= GATE_A_BLK2048 * 2
    lin = jnp.where(j == QKV_B // TILE_BLKS, MOBA_QSCALE, 1.0)

    acc = _dot_nt(h_ref[...], w_ref[...])

    @pl.when(jnp.logical_not(jnp.logical_or(is_silu, is_sig)))
    def _():
        proj_ref[...] = (acc * lin).astype(BF16)

    @pl.when(is_silu)
    def _():
        proj_ref[...] = (acc * _sigmoid(acc)).astype(BF16)

    @pl.when(is_sig)
    def _():
        proj_ref[...] = _sigmoid(acc).astype(BF16)


def _inproj(x2, nw, w_t, w_ba, alog_row, dtb_row):
    m, d = x2.shape
    tm, tn = INPROJ_TM, INPROJ_TN
    n = w_t.shape[0] - 2 * HEADS
    assert n % tn == 0 and N_LO_TILES * tn == 4 * HEADS * HEAD_DIM

    def w_rows(i, j):
        return (pl.multiple_of(j * tn + jnp.where(j >= N_LO_TILES, 2 * HEADS, 0), 2 * HEADS), 0)

    return pl.pallas_call(
        _inproj_kernel,
        grid=(m // tm, n // tn),
        in_specs=[
            pl.BlockSpec((tm, d), lambda i, j: (i, 0)),
            pl.BlockSpec((1, d), lambda i, j: (0, 0)),
            pl.BlockSpec((pl.Element(tn), pl.Element(d)), w_rows),
            pl.BlockSpec((LANES, d), lambda i, j: (0, 0)),
            pl.BlockSpec((1, LANES), lambda i, j: (0, 0)),
            pl.BlockSpec((1, LANES), lambda i, j: (0, 0)),
        ],
        out_specs=[
            pl.BlockSpec((tm, tn), lambda i, j: (i, j)),
            pl.BlockSpec((tm, LANES), lambda i, j: (i, 0)),
        ],
        out_shape=[
            jax.ShapeDtypeStruct((m, n), BF16),
            jax.ShapeDtypeStruct((m, LANES), F32),
        ],
        scratch_shapes=[pltpu.VMEM((tm, d), w_t.dtype)],
        compiler_params=pltpu.CompilerParams(
            dimension_semantics=("parallel", "arbitrary"), vmem_limit_bytes=VMEM_LIMIT),
        name="inproj",
    )(x2, nw, w_t, w_ba, alog_row, dtb_row)


def _interleave(*stages):
    live = list(stages)
    while live:
        for g in list(live):
            try:
                next(g)
            except StopIteration:
                live.remove(g)


def _stagger(*stages):
    live = []
    pending = list(stages)
    while live or pending:
        if pending:
            live.append(pending.pop(0))
        for g in list(live):
            try:
                next(g)
            except StopIteration:
                live.remove(g)
        yield


def _tri_inverse_group(lmats, row, col, out):
    c = lmats[0].shape[0]

    def sub_mask(s_log):
        bi = jnp.right_shift(row, s_log)
        bj = jnp.right_shift(col, s_log)
        return jnp.logical_and(jnp.bitwise_xor(bi, bj) == 1, jnp.bitwise_and(bi, 1) == 1)

    eye = (row == col).astype(F32)
    m0 = sub_mask(0)
    tds = [eye - jnp.where(m0, lm, 0.0) for lm in lmats]
    s_log = 1
    while (1 << s_log) < c:
        ms = sub_mask(s_log)
        tbs = [td.astype(BF16) for td in tds]
        x1s = [_dot(tb, jnp.where(ms, lm, 0.0).astype(BF16)).astype(BF16)
               for tb, lm in zip(tbs, lmats)]
        yield
        tds = [td - _dot(x1, tb) for td, x1, tb in zip(tds, x1s, tbs)]
        yield
        s_log += 1
    out.extend(tds)


def _mixers_kernel(q_ref, k_ref, v_ref, cwq_ref, cwk_ref, cwv_ref, bg_ref, zs_ref, nw_ref,
                   mq_ref, mk_ref, mv_ref, mzs_ref, o_ref, mo_ref,
                   xpad, qn, kn, vn, betab, gb, xs, cs, qps, olocs, gls):
    t = pl.program_id(0)

    @pl.when(t == 0)
    def _():
        xs[1] = jnp.zeros(xs.shape[1:], xs.dtype)
        cs[1] = jnp.zeros(cs.shape[1:], cs.dtype)
        qps[1] = jnp.zeros(qps.shape[1:], qps.dtype)
        olocs[1] = jnp.zeros(olocs.shape[1:], olocs.dtype)
        gls[1] = jnp.zeros(gls.shape[1:], gls.dtype)

    gdn_stages = _gdn_stages(q_ref, k_ref, v_ref, cwq_ref, cwk_ref, cwv_ref, bg_ref, zs_ref, nw_ref,
                             o_ref, xpad, qn, kn, vn, betab, gb, xs, cs, qps, olocs, gls)
    moba_stage = _moba_stages(mq_ref, mk_ref, mv_ref, mzs_ref, mo_ref)
    _interleave(*gdn_stages, moba_stage)


def _gdn_stages(q_ref, k_ref, v_ref, cwq_ref, cwk_ref, cwv_ref, bg_ref, zs_ref, nw_ref, o_ref,
                xpad, qn, kn, vn, betab, gb, xs, cs, qps, olocs, gls):
    s = q_ref.shape[0]
    c = GDN_CHUNK
    n_chunks = s // c
    t = pl.program_id(0)
    n_heads_total = pl.num_programs(0) - 1
    h = lax.rem(jnp.minimum(t, n_heads_total - 1), HEADS)
    cur = lax.rem(t, 2)
    prv = 1 - cur

    def conv_silu(src_ref, cw_ref):
        xpad[0:SUBLANES, :] = jnp.zeros((SUBLANES, HEAD_DIM), F32)
        xpad[SUBLANES:SUBLANES + s, :] = src_ref[...].astype(F32)
        base = SUBLANES - (CONV_WIDTH - 1)
        acc = cw_ref[0:1, :] * xpad[base:base + s, :]
        for j in range(1, CONV_WIDTH):
            acc = acc + cw_ref[j:j + 1, :] * xpad[base + j:base + j + s, :]
        half = 0.5 * acc
        return half * (1.0 + jnp.tanh(half))

    def l2n(x, scale=1.0):
        return x * (lax.rsqrt(jnp.sum(x * x, axis=-1, keepdims=True) + NORM_EPS) * scale)

    qn[...] = l2n(conv_silu(q_ref, cwq_ref), HEAD_DIM ** -0.5)
    kn[...] = l2n(conv_silu(k_ref, cwk_ref))
    vn[...] = conv_silu(v_ref, cwv_ref)

    bg = bg_ref[...]
    lane = lax.broadcasted_iota(jnp.int32, bg.shape, 1)
    beta_col = jnp.sum(jnp.where(lane == h, bg, 0.0), axis=-1, keepdims=True)
    g_col = jnp.sum(jnp.where(lane == h + HEADS, bg, 0.0), axis=-1, keepdims=True)
    betab[...] = jnp.broadcast_to(beta_col, bg.shape)
    gb[...] = jnp.broadcast_to(g_col, bg.shape)

    row = lax.broadcasted_iota(jnp.int32, (c, c), 0)
    col = lax.broadcasted_iota(jnp.int32, (c, c), 1)
    incl = row >= col
    strict = row > col
    tril_b = incl.astype(BF16)

    def cumsum_rows(x):
        x1 = x.astype(BF16)
        r1 = x - x1.astype(F32)
        x2 = r1.astype(BF16)
        x3 = (r1 - x2.astype(F32)).astype(BF16)
        return _dot(tril_b, x1) + _dot(tril_b, x2) + _dot(tril_b, x3)

    def prep_stage(ns):
        rows = [slice(n * c, (n + 1) * c) for n in ns]
        lane = lax.broadcasted_iota(jnp.int32, (c, LANES), 1)
        gcol = jnp.zeros((c, LANES), F32)
        for n, r in zip(ns, rows):
            gcol = jnp.where(lane == n, gb[r, :], gcol)
        csum = cumsum_rows(gcol)
        csum_t = csum.T
        yield
        gcs = [jnp.broadcast_to(csum[:, n:n + 1], (c, c)) for n in ns]
        decays = [jnp.exp(jnp.where(incl, gc - jnp.broadcast_to(csum_t[n:n + 1, :], (c, c)),
                                    NEG_INF)) for n, gc in zip(ns, gcs)]
        a2s = [_dot_nt(jnp.concatenate([kn[r, :] * betab[r, :], qn[r, :]], axis=0).astype(BF16),
                       kn[r, :].astype(BF16)) for r in rows]
        yield
        lmats = [jnp.where(strict, a2[:c] * d, 0.0) for a2, d in zip(a2s, decays)]
        ts = []
        yield from _tri_inverse_group(lmats, row, col, ts)
        uwbs = []
        for r, gc, tinv in zip(rows, gcs, ts):
            beta = betab[r, :]
            rhs = jnp.concatenate([kn[r, :] * beta * jnp.exp(gc), vn[r, :] * beta], axis=1)
            uwbs.append(_dot(tinv.astype(BF16), rhs.astype(BF16)).astype(BF16))
        yield
        for n, r, gc, a2, d, uwb in zip(ns, rows, gcs, a2s, decays, uwbs):
            g_last = gc[c - 1:c, :]
            intra = jnp.where(incl, a2[c:] * d, 0.0)
            kdec = (kn[r, :] * jnp.exp(g_last - gc)).astype(BF16)
            kwu = _dot_tn(kdec, uwb)
            iwu = _dot(intra.astype(BF16), uwb)
            xs[cur, n] = kwu[:, :HEAD_DIM].astype(BF16)
            cs[cur, n] = kwu[:, HEAD_DIM:]
            qps[cur, n] = (qn[r, :] * jnp.exp(gc) - iwu[:, :HEAD_DIM]).astype(BF16)
            olocs[cur, n] = iwu[:, HEAD_DIM:]
            gls[cur, n] = jnp.broadcast_to(jnp.exp(g_last), (SUBLANES, HEAD_DIM))
            if n % 4 == 3:
                yield

    def recurrence_stage():
        nw = nw_ref[...]
        state = jnp.zeros((HEAD_DIM, HEAD_DIM), F32)
        for n in range(n_chunks):
            rows = slice(n * c, (n + 1) * c)
            sb = state.astype(BF16)
            o = _dot(qps[prv, n], sb) + olocs[prv, n]
            o = o * lax.rsqrt(jnp.mean(o * o, axis=-1, keepdims=True) + NORM_EPS)
            o = o * nw * zs_ref[rows, :].astype(F32)
            o_ref[rows, :] = o.astype(BF16)
            state = state * gls[prv, n][0:1, :] + cs[prv, n] - _dot(xs[prv, n], sb)
            yield

    def prep_groups():
        for g0 in range(0, n_chunks, GDN_GROUP):
            yield from prep_stage(list(range(g0, g0 + GDN_GROUP)))

    return prep_groups(), recurrence_stage()


def _mixers(proj, conv_w, bg, norm_w, batch, seq):
    c = GDN_CHUNK
    n_chunks = seq // c
    n_heads_total = batch * HEADS
    rd = lambda t: jnp.minimum(t, n_heads_total - 1)
    wr = lambda t: jnp.maximum(t - 1, 0)
    blk = lambda off, hd: pl.BlockSpec(
        (seq, HEAD_DIM), lambda t: (hd(t) // HEADS, off + lax.rem(hd(t), HEADS)))
    cw = lambda off: pl.BlockSpec(
        (CONV_WIDTH, HEAD_DIM), lambda t: (0, off + lax.rem(rd(t), HEADS)))
    out = jax.ShapeDtypeStruct((batch * seq, HEADS * HEAD_DIM), BF16)
    return pl.pallas_call(
        _mixers_kernel,
        grid=(n_heads_total + 1,),
        in_specs=[
            blk(QKV_A, rd), blk(QKV_A + HEADS, rd), blk(QKV_A + 2 * HEADS, rd),
            cw(0), cw(HEADS), cw(2 * HEADS),
            pl.BlockSpec((seq, LANES), lambda t: (rd(t) // HEADS, 0)),
            blk(Z_A, wr),
            pl.BlockSpec((1, HEAD_DIM), lambda t: (0, 0)),
            blk(QKV_B, rd), blk(QKV_B + HEADS, rd), blk(QKV_B + 2 * HEADS, rd), blk(Z_B, rd),
        ],
        out_specs=[blk(0, wr), blk(0, rd)],
        out_shape=[out, out],
        scratch_shapes=[
            pltpu.VMEM((seq + SUBLANES, HEAD_DIM), F32),
            pltpu.VMEM((seq, HEAD_DIM), F32),
            pltpu.VMEM((seq, HEAD_DIM), F32),
            pltpu.VMEM((seq, HEAD_DIM), F32),
            pltpu.VMEM((seq, LANES), F32),
            pltpu.VMEM((seq, LANES), F32),
            pltpu.VMEM((2, n_chunks, HEAD_DIM, HEAD_DIM), BF16),
            pltpu.VMEM((2, n_chunks, HEAD_DIM, HEAD_DIM), F32),
            pltpu.VMEM((2, n_chunks, c, HEAD_DIM), BF16),
            pltpu.VMEM((2, n_chunks, c, HEAD_DIM), F32),
            pltpu.VMEM((2, n_chunks, SUBLANES, HEAD_DIM), F32),
        ],
        compiler_params=pltpu.CompilerParams(
            dimension_semantics=("arbitrary",), vmem_limit_bytes=VMEM_LIMIT),
        name="mixers",
    )(proj, proj, proj, conv_w, conv_w, conv_w, bg, proj, norm_w, proj, proj, proj, proj)


def _moba_stages(q_ref, k_ref, v_ref, zs_ref, o_ref):
    s = q_ref.shape[0]
    blk = MOBA_BLOCK
    nb = s // blk

    prow = lax.broadcasted_iota(jnp.int32, (LANES, s), 0)
    pcol = lax.broadcasted_iota(jnp.int32, (LANES, s), 1)
    pind = jnp.where(prow == pcol // blk, 1.0 / blk, 0.0).astype(BF16)
    kmean = _dot(pind, k_ref[...])
    cand = 2 * SUBLANES
    km_full = kmean.astype(BF16)
    km_hi = km_full[0:cand, :]
    km_lo = (kmean - km_full.astype(F32)).astype(BF16)[0:cand, :]

    sub = lax.broadcasted_iota(jnp.int32, (cand, blk), 0)
    key_i = lax.broadcasted_iota(jnp.int32, (blk, blk), 0)
    qry_i = lax.broadcasted_iota(jnp.int32, (blk, blk), 1)
    causal = key_i <= qry_i
    v_t = [v_ref[j * blk:(j + 1) * blk, :].astype(F32).T.astype(BF16) for j in range(nb)]

    def query_block(i):
        rows = slice(i * blk, (i + 1) * blk)
        q = q_ref[rows, :]
        pieces = [_dot_nt(k_ref[j * blk:(j + 1) * blk, :], q) for j in range(i + 1)]
        gt = _dot_nt(km_hi, q) + _dot_nt(km_lo, q) if i > MOBA_TOPK else None
        yield
        if i > MOBA_TOPK:
            rank = jnp.zeros((cand, blk), jnp.int32)
            for jp in range(i):
                gj = jnp.broadcast_to(gt[jp:jp + 1, :], (cand, blk))
                ahead = jnp.logical_or(gj > gt, jnp.logical_and(gj == gt, sub > jp))
                rank = rank + ahead.astype(jnp.int32)
            bias = jnp.where(rank >= MOBA_TOPK, NEG_INF, 0.0)
            for j in range(i):
                pieces[j] = pieces[j] + bias[j:j + 1, :]
        pieces[i] = jnp.where(causal, pieces[i], NEG_INF)
        mt = pieces[0]
        for pc in pieces[1:]:
            mt = jnp.maximum(mt, pc)
        m = jnp.max(mt, axis=0, keepdims=True)
        yield
        pt = None
        o_t = jnp.zeros((HEAD_DIM, blk), F32)
        for j, pc in enumerate(pieces):
            p = jnp.exp2(pc - m)
            pt = p if pt is None else pt + p
            o_t = o_t + _dot(v_t[j], p.astype(BF16))
            if j % 2 == 1:
                yield
        yield
        l = jnp.sum(pt, axis=0, keepdims=True)
        o = (o_t / l).T
        o_ref[rows, :] = (o * zs_ref[rows, :].astype(F32)).astype(BF16)

    order = [b for i in range(nb // 2) for b in (nb - 1 - i, i)]
    return _stagger(*[query_block(i) for i in order])


def _outproj_kernel(oa_ref, ob_ref, sa_ref, sb_ref, x_ref, wa_ref, wb_ref, wo_ref, pw_ref, o_ref):
    tm = o_ref.shape[0]
    halves = [slice(r, r + tm // OUT_SPLIT) for r in range(0, tm, tm // OUT_SPLIT)]
    ua = [_dot(oa_ref[r, :], wa_ref[...]) for r in halves]
    ub = [_dot(ob_ref[r, :], wb_ref[...]) for r in halves]
    merged = [(sa_ref[r, :].astype(F32) * a + sb_ref[r, :].astype(F32) * b).astype(BF16)
              for r, a, b in zip(halves, ua, ub)]
    ys = [_dot(mg, wo_ref[...]) for mg in merged]
    for r, y in zip(halves, ys):
        inv = lax.rsqrt(jnp.mean(y * y, axis=-1, keepdims=True) + NORM_EPS)
        o_ref[r, :] = x_ref[r, :] + y * inv * pw_ref[...]


def _outproj(oa, ob, proj, x2, wa, wb, wo, pw):
    m, d = x2.shape
    width = oa.shape[1]
    tm = OUT_TM
    const = lambda shape: pl.BlockSpec(shape, lambda i: (0, 0), pipeline_mode=pl.Buffered(1))
    return pl.pallas_call(
        _outproj_kernel,
        grid=(m // tm,),
        in_specs=[
            pl.BlockSpec((tm, width), lambda i: (i, 0)),
            pl.BlockSpec((tm, width), lambda i: (i, 0)),
            pl.BlockSpec((tm, d), lambda i: (i, GATE_A_BLK2048)),
            pl.BlockSpec((tm, d), lambda i: (i, GATE_B_BLK2048)),
            pl.BlockSpec((tm, d), lambda i: (i, 0)),
            const((width, d)), const((width, d)), const((d, d)), const((1, d)),
        ],
        out_specs=pl.BlockSpec((tm, d), lambda i: (i, 0)),
        out_shape=jax.ShapeDtypeStruct((m, d), F32),
        compiler_params=pltpu.CompilerParams(
            dimension_semantics=("parallel",), vmem_limit_bytes=VMEM_LIMIT),
        name="outproj",
    )(oa, ob, proj, proj, x2, wa, wb, wo, pw)


def _layer(x, pre_w, w_in, conv_w, a_log, dt_bias, gdn_norm_w, w_a, w_b, w_out, post_w):
    batch, seq, d = x.shape
    width = HEADS * HEAD_DIM
    ba_lo = 4 * width
    ba_hi = ba_lo + 2 * HEADS
    x2 = x.reshape(batch * seq, d)
    w_t = jnp.swapaxes(w_in, 0, 1)
    w_ba = jnp.pad(w_t[ba_lo:ba_hi, :], ((0, LANES - 2 * HEADS), (0, 0)))
    pad_row = lambda v: jnp.pad(v.astype(F32), (HEADS, LANES - 2 * HEADS)).reshape(1, LANES)
    proj, bg = _inproj(x2, pre_w.reshape(1, d), w_t, w_ba, pad_row(a_log), pad_row(dt_bias))
    oa, ob = _mixers(proj, conv_w, bg, gdn_norm_w.reshape(1, HEAD_DIM), batch, seq)
    out = _outproj(oa, ob, proj, x2, w_a.astype(BF16), w_b.astype(BF16), w_out.astype(BF16),
                   post_w.reshape(1, d))
    return out.reshape(batch, seq, d)


def kernel(x, pre_norm_w, w_in, conv_w, a_log, dt_bias, gdn_norm_w, w_branch_a, w_branch_b, w_out,
           post_norm_w):
    depth = pre_norm_w.shape[0]
    for l in range(depth):
        x = _layer(x, pre_norm_w[l], w_in[l], conv_w[l], a_log[l], dt_bias[l], gdn_norm_w[l],
                   w_branch_a[l], w_branch_b[l], w_out[l], post_norm_w[l])
    return x
```

```python
import functools

import jax
import jax.numpy as jnp
from jax import lax
from jax.experimental import pallas as pl
from jax.experimental.pallas import tpu as pltpu

F32 = jnp.float32
BF16 = jnp.bfloat16

NORM_EPS = 1e-6
NEG_INF = -1e30

HEADS = 8
HEAD_DIM = 128
CONV_WIDTH = 4
MOBA_BLOCK = 256
MOBA_TOPK = 3
GDN_CHUNK = 128
GDN_GROUP = 16

LANES = 128
SUBLANES = 8

QKV_A, Z_A, QKV_B, Z_B = 0, 24, 32, 56
GATE_A_BLK2048, GATE_B_BLK2048 = 4, 5

INPROJ_TM = 1024
INPROJ_TN = 1024
TILE_BLKS = INPROJ_TN // LANES
N_LO_TILES = (4 * HEADS * HEAD_DIM) // INPROJ_TN
OUT_TM = 512
OUT_SPLIT = 4

MOBA_QSCALE = (HEAD_DIM ** -0.5) * 1.4426950408889634

VMEM_LIMIT = 56 * 1024 * 1024


def _dot(a, b):
    return jnp.dot(a, b, preferred_element_type=F32)


def _dot_nt(a, b):
    return lax.dot_general(a, b, (((1,), (1,)), ((), ())), preferred_element_type=F32)


def _dot_tn(a, b):
    return lax.dot_general(a, b, (((0,), (0,)), ((), ())), preferred_element_type=F32)


def _sigmoid(x):
    return 0.5 * jnp.tanh(0.5 * x) + 0.5


def _softplus(x):
    return jnp.maximum(x, 0.0) + jnp.log1p(jnp.exp(-jnp.abs(x)))


def _inproj_kernel(x_ref, nw_ref, w_ref, wba_ref, alog_ref, dtb_ref, proj_ref, bg_ref, h_ref):
    j = pl.program_id(1)

    @pl.when(j == 0)
    def _():
        xf = x_ref[...]
        inv = lax.rsqrt(jnp.mean(xf * xf, axis=-1, keepdims=True) + NORM_EPS)
        h_ref[...] = (xf * inv * nw_ref[...]).astype(h_ref.dtype)
        ba = _dot_nt(h_ref[...], wba_ref[...])
        lane = lax.broadcasted_iota(jnp.int32, ba.shape, 1)
        beta = _sigmoid(ba)
        g = -jnp.exp(alog_ref[...]) * _softplus(ba + dtb_ref[...])
        bg_ref[...] = jnp.where(lane < HEADS, beta, g)

    is_silu = jnp.logical_or(j == Z_A // TILE_BLKS, j == Z_B // TILE_BLKS)
    is_sig = j >= GATE_A_BLK2048 * 2
    lin = jnp.where(j == QKV_B // TILE_BLKS, MOBA_QSCALE, 1.0)

    acc = _dot_nt(h_ref[...], w_ref[...])

    @pl.when(jnp.logical_not(jnp.logical_or(is_silu, is_sig)))
    def _():
        proj_ref[...] = (acc * lin).astype(BF16)

    @pl.when(is_silu)
    def _():
        proj_ref[...] = (acc * _sigmoid(acc)).astype(BF16)

    @pl.when(is_sig)
    def _():
        proj_ref[...] = _sigmoid(acc).astype(BF16)


def _inproj(x2, nw, w_t, w_ba, alog_row, dtb_row):
    m, d = x2.shape
    tm, tn = INPROJ_TM, INPROJ_TN
    n = w_t.shape[0] - 2 * HEADS
    assert n % tn == 0 and N_LO_TILES * tn == 4 * HEADS * HEAD_DIM

    def w_rows(i, j):
        return (pl.multiple_of(j * tn + jnp.where(j >= N_LO_TILES, 2 * HEADS, 0), 2 * HEADS), 0)

    return pl.pallas_call(
        _inproj_kernel,
        grid=(m // tm, n // tn),
        in_specs=[
            pl.BlockSpec((tm, d), lambda i, j: (i, 0)),
            pl.BlockSpec((1, d), lambda i, j: (0, 0)),
            pl.BlockSpec((pl.Element(tn), pl.Element(d)), w_rows),
            pl.BlockSpec((LANES, d), lambda i, j: (0, 0)),
            pl.BlockSpec((1, LANES), lambda i, j: (0, 0)),
            pl.BlockSpec((1, LANES), lambda i, j: (0, 0)),
        ],
        out_specs=[
            pl.BlockSpec((tm, tn), lambda i, j: (i, j)),
            pl.BlockSpec((tm, LANES), lambda i, j: (i, 0)),
        ],
        out_shape=[
            jax.ShapeDtypeStruct((m, n), BF16),
            jax.ShapeDtypeStruct((m, LANES), F32),
        ],
        scratch_shapes=[pltpu.VMEM((tm, d), w_t.dtype)],
        compiler_params=pltpu.CompilerParams(
            dimension_semantics=("parallel", "arbitrary"), vmem_limit_bytes=VMEM_LIMIT),
        name="inproj",
    )(x2, nw, w_t, w_ba, alog_row, dtb_row)


def _interleave(*stages):
    live = list(stages)
    while live:
        for g in list(live):
            try:
                next(g)
            except StopIteration:
                live.remove(g)


def _stagger(*stages):
    live = []
    pending = list(stages)
    while live or pending:
        if pending:
            live.append(pending.pop(0))
        for g in list(live):
            try:
                next(g)
            except StopIteration:
                live.remove(g)
        yield


def _tri_inverse_group(lmats, row, col, out):
    c = lmats[0].shape[0]

    def sub_mask(s_log):
        bi = jnp.right_shift(row, s_log)
        bj = jnp.right_shift(col, s_log)
        return jnp.logical_and(jnp.bitwise_xor(bi, bj) == 1, jnp.bitwise_and(bi, 1) == 1)

    eye = (row == col).astype(BF16)
    lbs = [lm.astype(BF16) for lm in lmats]
    m0 = sub_mask(0).astype(BF16)
    tds = [eye - lb * m0 for lb in lbs]
    s_log = 1
    while (1 << s_log) < c:
        ms = sub_mask(s_log).astype(BF16)
        x1s = [_dot(td, lb * ms).astype(BF16) for td, lb in zip(tds, lbs)]
        yield
        tds = [td - _dot(x1, td).astype(BF16) for td, x1 in zip(tds, x1s)]
        yield
        s_log += 1
    out.extend(tds)


def _mixers_kernel(q_ref, k_ref, v_ref, cwq_ref, cwk_ref, cwv_ref, bg_ref, zs_ref, nw_ref,
                   mq_ref, mk_ref, mv_ref, mzs_ref, o_ref, mo_ref,
                   xpad, qn, kn, vn, betab, gb, xs, cs, qps, olocs, gls):
    t = pl.program_id(0)

    @pl.when(t == 0)
    def _():
        xs[1] = jnp.zeros(xs.shape[1:], xs.dtype)
        cs[1] = jnp.zeros(cs.shape[1:], cs.dtype)
        qps[1] = jnp.zeros(qps.shape[1:], qps.dtype)
        olocs[1] = jnp.zeros(olocs.shape[1:], olocs.dtype)
        gls[1] = jnp.zeros(gls.shape[1:], gls.dtype)

    gdn_stages = _gdn_stages(q_ref, k_ref, v_ref, cwq_ref, cwk_ref, cwv_ref, bg_ref, zs_ref, nw_ref,
                             o_ref, xpad, qn, kn, vn, betab, gb, xs, cs, qps, olocs, gls)
    moba_stage = _moba_stages(mq_ref, mk_ref, mv_ref, mzs_ref, mo_ref)
    _interleave(*gdn_stages, moba_stage)


def _gdn_stages(q_ref, k_ref, v_ref, cwq_ref, cwk_ref, cwv_ref, bg_ref, zs_ref, nw_ref, o_ref,
                xpad, qn, kn, vn, betab, gb, xs, cs, qps, olocs, gls):
    s = q_ref.shape[0]
    c = GDN_CHUNK
    n_chunks = s // c
    t = pl.program_id(0)
    n_heads_total = pl.num_programs(0) - 1
    h = lax.rem(jnp.minimum(t, n_heads_total - 1), HEADS)
    cur = lax.rem(t, 2)
    prv = 1 - cur

    def conv_silu(src_ref, cw_ref):
        xpad[0:SUBLANES, :] = jnp.zeros((SUBLANES, HEAD_DIM), F32)
        xpad[SUBLANES:SUBLANES + s, :] = src_ref[...].astype(F32)
        base = SUBLANES - (CONV_WIDTH - 1)
        acc = cw_ref[0:1, :] * xpad[base:base + s, :]
        for j in range(1, CONV_WIDTH):
            acc = acc + cw_ref[j:j + 1, :] * xpad[base + j:base + j + s, :]
        half = 0.5 * acc
        return half * (1.0 + jnp.tanh(half))

    def l2n(x, scale=1.0):
        return x * (lax.rsqrt(jnp.sum(x * x, axis=-1, keepdims=True) + NORM_EPS) * scale)

    qn[...] = l2n(conv_silu(q_ref, cwq_ref), HEAD_DIM ** -0.5)
    kn[...] = l2n(conv_silu(k_ref, cwk_ref))
    vn[...] = conv_silu(v_ref, cwv_ref)

    bg = bg_ref[...]
    lane = lax.broadcasted_iota(jnp.int32, bg.shape, 1)
    beta_col = jnp.sum(jnp.where(lane == h, bg, 0.0), axis=-1, keepdims=True)
    g_col = jnp.sum(jnp.where(lane == h + HEADS, bg, 0.0), axis=-1, keepdims=True)
    betab[...] = jnp.broadcast_to(beta_col, bg.shape)
    gb[...] = jnp.broadcast_to(g_col, bg.shape)

    row = lax.broadcasted_iota(jnp.int32, (c, c), 0)
    col = lax.broadcasted_iota(jnp.int32, (c, c), 1)
    incl = row >= col
    strict = row > col
    tril_b = incl.astype(BF16)

    def cumsum_rows(x):
        x1 = x.astype(BF16)
        r1 = x - x1.astype(F32)
        x2 = r1.astype(BF16)
        x3 = (r1 - x2.astype(F32)).astype(BF16)
        return _dot(tril_b, x1) + _dot(tril_b, x2) + _dot(tril_b, x3)

    def prep_stage(ns):
        rows = [slice(n * c, (n + 1) * c) for n in ns]
        lane = lax.broadcasted_iota(jnp.int32, (c, LANES), 1)
        gcol = jnp.zeros((c, LANES), F32)
        for n, r in zip(ns, rows):
            gcol = jnp.where(lane == n, gb[r, :], gcol)
        csum = cumsum_rows(gcol)
        csum_t = csum.T
        yield
        gcs = [jnp.broadcast_to(csum[:, n:n + 1], (c, c)) for n in ns]
        decays = [jnp.exp(jnp.where(incl, gc - jnp.broadcast_to(csum_t[n:n + 1, :], (c, c)),
                                    NEG_INF)) for n, gc in zip(ns, gcs)]
        a2s = [_dot_nt(jnp.concatenate([kn[r, :] * betab[r, :], qn[r, :]], axis=0).astype(BF16),
                       kn[r, :].astype(BF16)) for r in rows]
        yield
        lmats = [jnp.where(strict, a2[:c] * d, 0.0) for a2, d in zip(a2s, decays)]
        ts = []
        yield from _tri_inverse_group(lmats, row, col, ts)
        uwbs = []
        for r, gc, tinv in zip(rows, gcs, ts):
            beta = betab[r, :]
            rhs = jnp.concatenate([kn[r, :] * beta * jnp.exp(gc), vn[r, :] * beta], axis=1)
            uwbs.append(_dot(tinv.astype(BF16), rhs.astype(BF16)).astype(BF16))
        yield
        for n, r, gc, a2, d, uwb in zip(ns, rows, gcs, a2s, decays, uwbs):
            g_last = gc[c - 1:c, :]
            intra = jnp.where(incl, a2[c:] * d, 0.0)
            kdec = (kn[r, :] * jnp.exp(g_last - gc)).astype(BF16)
            kwu = _dot_tn(kdec, uwb)
            iwu = _dot(intra.astype(BF16), uwb)
            xs[cur, n] = kwu[:, :HEAD_DIM].astype(BF16)
            cs[cur, n] = kwu[:, HEAD_DIM:]
            qps[cur, n] = (qn[r, :] * jnp.exp(gc) - iwu[:, :HEAD_DIM]).astype(BF16)
            olocs[cur, n] = iwu[:, HEAD_DIM:]
            gls[cur, n] = jnp.broadcast_to(jnp.exp(g_last), (SUBLANES, HEAD_DIM))
            if n % 4 == 3:
                yield

    def recurrence_stage():
        nw = nw_ref[...]
        state = jnp.zeros((HEAD_DIM, HEAD_DIM), F32)
        for n in range(n_chunks):
            rows = slice(n * c, (n + 1) * c)
            sb = state.astype(BF16)
            o = _dot(qps[prv, n], sb) + olocs[prv, n]
            o = o * lax.rsqrt(jnp.mean(o * o, axis=-1, keepdims=True) + NORM_EPS)
            o = o * nw * zs_ref[rows, :].astype(F32)
            o_ref[rows, :] = o.astype(BF16)
            state = state * gls[prv, n][0:1, :] + cs[prv, n] - _dot(xs[prv, n], sb)
            yield

    def prep_groups():
        for g0 in range(0, n_chunks, GDN_GROUP):
            yield from prep_stage(list(range(g0, g0 + GDN_GROUP)))

    return prep_groups(), recurrence_stage()


def _mixers(proj, conv_w, bg, norm_w, batch, seq):
    c = GDN_CHUNK
    n_chunks = seq // c
    n_heads_total = batch * HEADS
    rd = lambda t: jnp.minimum(t, n_heads_total - 1)
    wr = lambda t: jnp.maximum(t - 1, 0)
    blk = lambda off, hd: pl.BlockSpec(
        (seq, HEAD_DIM), lambda t: (hd(t) // HEADS, off + lax.rem(hd(t), HEADS)))
    cw = lambda off: pl.BlockSpec(
        (CONV_WIDTH, HEAD_DIM), lambda t: (0, off + lax.rem(rd(t), HEADS)))
    out = jax.ShapeDtypeStruct((batch * seq, HEADS * HEAD_DIM), BF16)
    return pl.pallas_call(
        _mixers_kernel,
        grid=(n_heads_total + 1,),
        in_specs=[
            blk(QKV_A, rd), blk(QKV_A + HEADS, rd), blk(QKV_A + 2 * HEADS, rd),
            cw(0), cw(HEADS), cw(2 * HEADS),
            pl.BlockSpec((seq, LANES), lambda t: (rd(t) // HEADS, 0)),
            blk(Z_A, wr),
            pl.BlockSpec((1, HEAD_DIM), lambda t: (0, 0)),
            blk(QKV_B, rd), blk(QKV_B + HEADS, rd), blk(QKV_B + 2 * HEADS, rd), blk(Z_B, rd),
        ],
        out_specs=[blk(0, wr), blk(0, rd)],
        out_shape=[out, out],
        scratch_shapes=[
            pltpu.VMEM((seq + SUBLANES, HEAD_DIM), F32),
            pltpu.VMEM((seq, HEAD_DIM), F32),
            pltpu.VMEM((seq, HEAD_DIM), F32),
            pltpu.VMEM((seq, HEAD_DIM), F32),
            pltpu.VMEM((seq, LANES), F32),
            pltpu.VMEM((seq, LANES), F32),
            pltpu.VMEM((2, n_chunks, HEAD_DIM, HEAD_DIM), BF16),
            pltpu.VMEM((2, n_chunks, HEAD_DIM, HEAD_DIM), F32),
            pltpu.VMEM((2, n_chunks, c, HEAD_DIM), BF16),
            pltpu.VMEM((2, n_chunks, c, HEAD_DIM), F32),
            pltpu.VMEM((2, n_chunks, SUBLANES, HEAD_DIM), F32),
        ],
        compiler_params=pltpu.CompilerParams(
            dimension_semantics=("arbitrary",), vmem_limit_bytes=VMEM_LIMIT),
        name="mixers",
    )(proj, proj, proj, conv_w, conv_w, conv_w, bg, proj, norm_w, proj, proj, proj, proj)


def _moba_stages(q_ref, k_ref, v_ref, zs_ref, o_ref):
    s = q_ref.shape[0]
    blk = MOBA_BLOCK
    nb = s // blk

    prow = lax.broadcasted_iota(jnp.int32, (LANES, s), 0)
    pcol = lax.broadcasted_iota(jnp.int32, (LANES, s), 1)
    pind = jnp.where(prow == pcol // blk, 1.0 / blk, 0.0).astype(BF16)
    kmean = _dot(pind, k_ref[...])
    cand = 2 * SUBLANES
    km_full = kmean.astype(BF16)
    km_hi = km_full[0:cand, :]
    km_lo = (kmean - km_full.astype(F32)).astype(BF16)[0:cand, :]

    sub = lax.broadcasted_iota(jnp.int32, (cand, blk), 0)
    key_i = lax.broadcasted_iota(jnp.int32, (blk, blk), 0)
    qry_i = lax.broadcasted_iota(jnp.int32, (blk, blk), 1)
    causal = key_i <= qry_i
    v_t = [v_ref[j * blk:(j + 1) * blk, :].astype(F32).T.astype(BF16) for j in range(nb)]

    def query_block(i):
        rows = slice(i * blk, (i + 1) * blk)
        q = q_ref[rows, :]
        pieces = [_dot_nt(k_ref[j * blk:(j + 1) * blk, :], q) for j in range(i + 1)]
        gt = _dot_nt(km_hi, q) + _dot_nt(km_lo, q) if i > MOBA_TOPK else None
        yield
        if i > MOBA_TOPK:
            rank = jnp.zeros((cand, blk), jnp.int32)
            for jp in range(i):
                gj = jnp.broadcast_to(gt[jp:jp + 1, :], (cand, blk))
                ahead = jnp.logical_or(gj > gt, jnp.logical_and(gj == gt, sub > jp))
                rank = rank + ahead.astype(jnp.int32)
            bias = jnp.where(rank >= MOBA_TOPK, NEG_INF, 0.0)
            for j in range(i):
                pieces[j] = pieces[j] + bias[j:j + 1, :]
        pieces[i] = jnp.where(causal, pieces[i], NEG_INF)
        mt = pieces[0]
        for pc in pieces[1:]:
            mt = jnp.maximum(mt, pc)
        m = jnp.max(mt, axis=0, keepdims=True)
        yield
        pt = None
        o_t = jnp.zeros((HEAD_DIM, blk), F32)
        for j, pc in enumerate(pieces):
            p = jnp.exp2(pc - m)
            pt = p if pt is None else pt + p
            o_t = o_t + _dot(v_t[j], p.astype(BF16))
            if j % 2 == 1:
                yield
        yield
        l = jnp.sum(pt, axis=0, keepdims=True)
        o = (o_t / l).T
        o_ref[rows, :] = (o * zs_ref[rows, :].astype(F32)).astype(BF16)

    order = [b for i in range(nb // 2) for b in (nb - 1 - i, i)]
    return _stagger(*[query_block(i) for i in order])


def _outproj_kernel(oa_ref, ob_ref, sa_ref, sb_ref, x_ref, wa_ref, wb_ref, wo_ref, pw_ref, o_ref):
    tm = o_ref.shape[0]
    halves = [slice(r, r + tm // OUT_SPLIT) for r in range(0, tm, tm // OUT_SPLIT)]
    ua = [_dot(oa_ref[r, :], wa_ref[...]) for r in halves]
    ub = [_dot(ob_ref[r, :], wb_ref[...]) for r in halves]
    merged = [(sa_ref[r, :].astype(F32) * a + sb_ref[r, :].astype(F32) * b).astype(BF16)
              for r, a, b in zip(halves, ua, ub)]
    ys = [_dot(mg, wo_ref[...]) for mg in merged]
    for r, y in zip(halves, ys):
        inv = lax.rsqrt(jnp.mean(y * y, axis=-1, keepdims=True) + NORM_EPS)
        o_ref[r, :] = x_ref[r, :] + y * inv * pw_ref[...]


def _outproj(oa, ob, proj, x2, wa, wb, wo, pw):
    m, d = x2.shape
    width = oa.shape[1]
    tm = OUT_TM
    const = lambda shape: pl.BlockSpec(shape, lambda i: (0, 0), pipeline_mode=pl.Buffered(1))
    return pl.pallas_call(
        _outproj_kernel,
        grid=(m // tm,),
        in_specs=[
            pl.BlockSpec((tm, width), lambda i: (i, 0)),
            pl.BlockSpec((tm, width), lambda i: (i, 0)),
            pl.BlockSpec((tm, d), lambda i: (i, GATE_A_BLK2048)),
            pl.BlockSpec((tm, d), lambda i: (i, GATE_B_BLK2048)),
            pl.BlockSpec((tm, d), lambda i: (i, 0)),
            const((width, d)), const((width, d)), const((d, d)), const((1, d)),
        ],
        out_specs=pl.BlockSpec((tm, d), lambda i: (i, 0)),
        out_shape=jax.ShapeDtypeStruct((m, d), F32),
        compiler_params=pltpu.CompilerParams(
            dimension_semantics=("parallel",), vmem_limit_bytes=VMEM_LIMIT),
        name="outproj",
    )(oa, ob, proj, proj, x2, wa, wb, wo, pw)


def _layer(x, pre_w, w_in, conv_w, a_log, dt_bias, gdn_norm_w, w_a, w_b, w_out, post_w):
    batch, seq, d = x.shape
    width = HEADS * HEAD_DIM
    ba_lo = 4 * width
    ba_hi = ba_lo + 2 * HEADS
    x2 = x.reshape(batch * seq, d)
    w_t = jnp.swapaxes(w_in, 0, 1)
    w_ba = jnp.pad(w_t[ba_lo:ba_hi, :], ((0, LANES - 2 * HEADS), (0, 0)))
    pad_row = lambda v: jnp.pad(v.astype(F32), (HEADS, LANES - 2 * HEADS)).reshape(1, LANES)
    proj, bg = _inproj(x2, pre_w.reshape(1, d), w_t, w_ba, pad_row(a_log), pad_row(dt_bias))
    oa, ob = _mixers(proj, conv_w, bg, gdn_norm_w.reshape(1, HEAD_DIM), batch, seq)
    out = _outproj(oa, ob, proj, x2, w_a.astype(BF16), w_b.astype(BF16), w_out.astype(BF16),
                   post_w.reshape(1, d))
    return out.reshape(batch, seq, d)


def kernel(x, pre_norm_w, w_in, conv_w, a_log, dt_bias, gdn_norm_w, w_branch_a, w_branch_b, w_out,
           post_norm_w):
    depth = pre_norm_w.shape[0]
    for l in range(depth):
        x = _layer(x, pre_norm_w[l], w_in[l], conv_w[l], a_log[l], dt_bias[l], gdn_norm_w[l],
                   w_branch_a[l], w_branch_b[l], w_out[l], post_norm_w[l])
    return x
```

```python
import jax
import jax.numpy as jnp
from jax import lax
from jax.experimental import pallas as pl
from jax.experimental.pallas import tpu as pltpu

F32 = jnp.float32
BF16 = jnp.bfloat16

NORM_EPS = 1e-6
NEG_INF = -1e30

HEADS = 8
HEAD_DIM = 128
CONV_WIDTH = 4
MOBA_BLOCK = 256
MOBA_TOPK = 3
GDN_CHUNK = 128
GDN_GROUP = 16
STAGGER_PERIOD = 1

LANES = 128
SUBLANES = 8

QKV_A, Z_A, QKV_B, Z_B = 0, 24, 32, 56
GATE_A_BLK2048, GATE_B_BLK2048 = 4, 5

INPROJ_TM = 1024
INPROJ_TN = 1024
TILE_BLKS = INPROJ_TN // LANES
N_LO_TILES = (4 * HEADS * HEAD_DIM) // INPROJ_TN
OUT_TM = 512
OUT_SPLIT = 4

MOBA_QSCALE = (HEAD_DIM ** -0.5) * 1.4426950408889634

VMEM_LIMIT = 56 * 1024 * 1024


def _dot(a, b):
    return jnp.dot(a, b, preferred_element_type=F32)


def _dot_nt(a, b):
    return lax.dot_general(a, b, (((1,), (1,)), ((), ())), preferred_element_type=F32)


def _dot_tn(a, b):
    return lax.dot_general(a, b, (((0,), (0,)), ((), ())), preferred_element_type=F32)


def _sigmoid(x):
    return 0.5 * jnp.tanh(0.5 * x) + 0.5


def _softplus(x):
    return jnp.maximum(x, 0.0) + jnp.log1p(jnp.exp(-jnp.abs(x)))


def _inproj_kernel(x_ref, nw_ref, w_ref, wba_ref, alog_ref, dtb_ref, proj_ref, bg_ref, h_ref):
    j = pl.program_id(1)

    @pl.when(j == 0)
    def _():
        xf = x_ref[...]
        inv = lax.rsqrt(jnp.mean(xf * xf, axis=-1, keepdims=True) + NORM_EPS)
        h_ref[...] = (xf * inv * nw_ref[...]).astype(h_ref.dtype)
        ba = _dot_nt(h_ref[...], wba_ref[...])
        lane = lax.broadcasted_iota(jnp.int32, ba.shape, 1)
        beta = _sigmoid(ba)
        g = -jnp.exp(alog_ref[...]) * _softplus(ba + dtb_ref[...])
        bg_ref[...] = jnp.where(lane < HEADS, beta, g)

    is_silu = jnp.logical_or(j == Z_A // TILE_BLKS, j == Z_B // TILE_BLKS)
    is_sig = j >= GATE_A_BLK2048 * 2
    lin = jnp.where(j == QKV_B // TILE_BLKS, MOBA_QSCALE, 1.0)

    acc = _dot_nt(h_ref[...], w_ref[...])

    @pl.when(jnp.logical_not(jnp.logical_or(is_silu, is_sig)))
    def _():
        proj_ref[...] = (acc * lin).astype(BF16)

    @pl.when(is_silu)
    def _():
        half = 0.5 * acc
        proj_ref[...] = (half * (1.0 + jnp.tanh(half))).astype(BF16)

    @pl.when(is_sig)
    def _():
        proj_ref[...] = _sigmoid(acc).astype(BF16)


def _inproj(x2, nw, w_t, w_ba, alog_row, dtb_row):
    m, d = x2.shape
    tm, tn = INPROJ_TM, INPROJ_TN
    n = w_t.shape[0] - 2 * HEADS
    assert n % tn == 0 and N_LO_TILES * tn == 4 * HEADS * HEAD_DIM

    def w_rows(i, j):
        return (pl.multiple_of(j * tn + jnp.where(j >= N_LO_TILES, 2 * HEADS, 0), 2 * HEADS), 0)

    return pl.pallas_call(
        _inproj_kernel,
        grid=(m // tm, n // tn),
        in_specs=[
            pl.BlockSpec((tm, d), lambda i, j: (i, 0)),
            pl.BlockSpec((1, d), lambda i, j: (0, 0)),
            pl.BlockSpec((pl.Element(tn), pl.Element(d)), w_rows),
            pl.BlockSpec((LANES, d), lambda i, j: (0, 0)),
            pl.BlockSpec((1, LANES), lambda i, j: (0, 0)),
            pl.BlockSpec((1, LANES), lambda i, j: (0, 0)),
        ],
        out_specs=[
            pl.BlockSpec((tm, tn), lambda i, j: (i, j)),
            pl.BlockSpec((tm, LANES), lambda i, j: (i, 0)),
        ],
        out_shape=[
            jax.ShapeDtypeStruct((m, n), BF16),
            jax.ShapeDtypeStruct((m, LANES), F32),
        ],
        scratch_shapes=[pltpu.VMEM((tm, d), BF16)],
        compiler_params=pltpu.CompilerParams(
            dimension_semantics=("parallel", "arbitrary"), vmem_limit_bytes=VMEM_LIMIT),
        name="inproj",
    )(x2, nw, w_t, w_ba, alog_row, dtb_row)


def _interleave(*stages):
    live = list(stages)
    while live:
        for g in list(live):
            try:
                next(g)
            except StopIteration:
                live.remove(g)


def _stagger(*stages):
    live = []
    pending = list(stages)
    rnd = 0
    while live or pending:
        if pending and rnd % STAGGER_PERIOD == 0:
            live.append(pending.pop(0))
        rnd += 1
        for g in list(live):
            try:
                next(g)
            except StopIteration:
                live.remove(g)
        yield


def _tri_inverse_group(lmats, row, col, out):
    c = lmats[0].shape[0]

    def sub_mask(s_log):
        bi = jnp.right_shift(row, s_log)
        bj = jnp.right_shift(col, s_log)
        return jnp.logical_and(jnp.bitwise_xor(bi, bj) == 1, jnp.bitwise_and(bi, 1) == 1)

    eye = (row == col).astype(BF16)
    lbs = [lm.astype(BF16) for lm in lmats]
    m0 = sub_mask(0).astype(BF16)
    tds = [eye - lb * m0 for lb in lbs]
    s_log = 1
    while (1 << s_log) < c:
        ms = sub_mask(s_log).astype(BF16)
        x1s = [_dot(td, lb * ms).astype(BF16) for td, lb in zip(tds, lbs)]
        yield
        tds = [td - _dot(x1, td).astype(BF16) for td, x1 in zip(tds, x1s)]
        yield
        s_log += 1
    out.extend(tds)


def _mixers_kernel(q_ref, k_ref, v_ref, cwq_ref, cwk_ref, cwv_ref, bg_ref, zs_ref, nw_ref,
                   mq_ref, mk_ref, mv_ref, mzs_ref, o_ref, mo_ref,
                   xpad, qn, kn, vn, betab, gb, xs, cs, qps, olocs, gls):
    t = pl.program_id(0)

    @pl.when(t == 0)
    def _():
        xs[1] = jnp.zeros(xs.shape[1:], xs.dtype)
        cs[1] = jnp.zeros(cs.shape[1:], cs.dtype)
        qps[1] = jnp.zeros(qps.shape[1:], qps.dtype)
        olocs[1] = jnp.zeros(olocs.shape[1:], olocs.dtype)
        gls[1] = jnp.zeros(gls.shape[1:], gls.dtype)

    gdn_stages = _gdn_stages(q_ref, k_ref, v_ref, cwq_ref, cwk_ref, cwv_ref, bg_ref, zs_ref, nw_ref,
                             o_ref, xpad, qn, kn, vn, betab, gb, xs, cs, qps, olocs, gls)
    moba_stage = _moba_stages(mq_ref, mk_ref, mv_ref, mzs_ref, mo_ref)
    _interleave(*gdn_stages, moba_stage)


def _gdn_stages(q_ref, k_ref, v_ref, cwq_ref, cwk_ref, cwv_ref, bg_ref, zs_ref, nw_ref, o_ref,
                xpad, qn, kn, vn, betab, gb, xs, cs, qps, olocs, gls):
    s = q_ref.shape[0]
    c = GDN_CHUNK
    n_chunks = s // c
    t = pl.program_id(0)
    n_heads_total = pl.num_programs(0) - 1
    h = lax.rem(jnp.minimum(t, n_heads_total - 1), HEADS)
    cur = lax.rem(t, 2)
    prv = 1 - cur

    def conv_silu(src_ref, cw_ref):
        xpad[0:SUBLANES, :] = jnp.zeros((SUBLANES, HEAD_DIM), F32)
        xpad[SUBLANES:SUBLANES + s, :] = src_ref[...].astype(F32)
        base = SUBLANES - (CONV_WIDTH - 1)
        acc = cw_ref[0:1, :] * xpad[base:base + s, :]
        for j in range(1, CONV_WIDTH):
            acc = acc + cw_ref[j:j + 1, :] * xpad[base + j:base + j + s, :]
        half = 0.5 * acc
        return half * (1.0 + jnp.tanh(half))

    def l2n(x, scale=1.0):
        return x * (lax.rsqrt(jnp.sum(x * x, axis=-1, keepdims=True) + NORM_EPS) * scale)

    def conv_steps():
        qn[...] = l2n(conv_silu(q_ref, cwq_ref), HEAD_DIM ** -0.5)
        yield
        kn[...] = l2n(conv_silu(k_ref, cwk_ref))
        yield
        vn[...] = conv_silu(v_ref, cwv_ref)
        yield
        bg = bg_ref[...]
        lane = lax.broadcasted_iota(jnp.int32, bg.shape, 1)
        beta_col = jnp.sum(jnp.where(lane == h, bg, 0.0), axis=-1, keepdims=True)
        g_col = jnp.sum(jnp.where(lane == h + HEADS, bg, 0.0), axis=-1, keepdims=True)
        betab[...] = jnp.broadcast_to(beta_col, bg.shape)
        gb[...] = jnp.broadcast_to(g_col, bg.shape)
        yield

    row = lax.broadcasted_iota(jnp.int32, (c, c), 0)
    col = lax.broadcasted_iota(jnp.int32, (c, c), 1)
    incl = row >= col
    strict = row > col
    tril_b = incl.astype(BF16)

    def cumsum_rows(x):
        x1 = x.astype(BF16)
        r1 = x - x1.astype(F32)
        x2 = r1.astype(BF16)
        x3 = (r1 - x2.astype(F32)).astype(BF16)
        return _dot(tril_b, x1) + _dot(tril_b, x2) + _dot(tril_b, x3)

    def prep_stage(ns):
        rows = [slice(n * c, (n + 1) * c) for n in ns]
        lane = lax.broadcasted_iota(jnp.int32, (c, LANES), 1)
        gcol = jnp.zeros((c, LANES), F32)
        for n, r in zip(ns, rows):
            gcol = jnp.where(lane == n, gb[r, :], gcol)
        csum = cumsum_rows(gcol)
        csum_t = csum.T
        yield
        gcs = [jnp.broadcast_to(csum[:, n:n + 1], (c, c)) for n in ns]
        decays = [jnp.exp(jnp.where(incl, gc - jnp.broadcast_to(csum_t[n:n + 1, :], (c, c)),
                                    NEG_INF)) for n, gc in zip(ns, gcs)]
        a2s = [_dot_nt(jnp.concatenate([kn[r, :] * betab[r, :], qn[r, :]], axis=0).astype(BF16),
                       kn[r, :].astype(BF16)) for r in rows]
        yield
        lmats = [jnp.where(strict, a2[:c] * d, 0.0) for a2, d in zip(a2s, decays)]
        ts = []
        yield from _tri_inverse_group(lmats, row, col, ts)
        uwbs = []
        for r, gc, tinv in zip(rows, gcs, ts):
            beta = betab[r, :]
            rhs = jnp.concatenate([kn[r, :] * beta * jnp.exp(gc), vn[r, :] * beta], axis=1)
            uwbs.append(_dot(tinv.astype(BF16), rhs.astype(BF16)).astype(BF16))
        yield
        for n, r, gc, a2, d, uwb in zip(ns, rows, gcs, a2s, decays, uwbs):
            g_last = gc[c - 1:c, :]
            intra = jnp.where(incl, a2[c:] * d, 0.0)
            kdec = (kn[r, :] * jnp.exp(g_last - gc)).astype(BF16)
            kwu = _dot_tn(kdec, uwb)
            iwu = _dot(intra.astype(BF16), uwb)
            xs[cur, n] = kwu[:, :HEAD_DIM].astype(BF16)
            cs[cur, n] = kwu[:, HEAD_DIM:]
            qps[cur, n] = (qn[r, :] * jnp.exp(gc) - iwu[:, :HEAD_DIM]).astype(BF16)
            olocs[cur, n] = iwu[:, HEAD_DIM:]
            gls[cur, n] = jnp.broadcast_to(jnp.exp(g_last), (SUBLANES, HEAD_DIM))
            if n % 4 == 3:
                yield

    def recurrence_stage():
        nw = nw_ref[...]
        state = jnp.zeros((HEAD_DIM, HEAD_DIM), F32)
        for n in range(n_chunks):
            rows = slice(n * c, (n + 1) * c)
            sb = state.astype(BF16)
            o = _dot(qps[prv, n], sb) + olocs[prv, n]
            o = o * lax.rsqrt(jnp.mean(o * o, axis=-1, keepdims=True) + NORM_EPS)
            o = o * nw * zs_ref[rows, :].astype(F32)
            o_ref[rows, :] = o.astype(BF16)
            state = state * gls[prv, n][0:1, :] + cs[prv, n] - _dot(xs[prv, n], sb)
            yield

    for _ in conv_steps():
        pass

    def prep_groups():
        for g0 in range(0, n_chunks, GDN_GROUP):
            yield from prep_stage(list(range(g0, g0 + GDN_GROUP)))

    return prep_groups(), recurrence_stage()


def _mixers(proj, conv_w, bg, norm_w, batch, seq):
    c = GDN_CHUNK
    n_chunks = seq // c
    n_heads_total = batch * HEADS
    rd = lambda t: jnp.minimum(t, n_heads_total - 1)
    wr = lambda t: jnp.maximum(t - 1, 0)
    blk = lambda off, hd: pl.BlockSpec(
        (seq, HEAD_DIM), lambda t: (hd(t) // HEADS, off + lax.rem(hd(t), HEADS)))
    cw = lambda off: pl.BlockSpec(
        (CONV_WIDTH, HEAD_DIM), lambda t: (0, off + lax.rem(rd(t), HEADS)))
    out = jax.ShapeDtypeStruct((batch * seq, HEADS * HEAD_DIM), BF16)
    return pl.pallas_call(
        _mixers_kernel,
        grid=(n_heads_total + 1,),
        in_specs=[
            blk(QKV_A, rd), blk(QKV_A + HEADS, rd), blk(QKV_A + 2 * HEADS, rd),
            cw(0), cw(HEADS), cw(2 * HEADS),
            pl.BlockSpec((seq, LANES), lambda t: (rd(t) // HEADS, 0)),
            blk(Z_A, wr),
            pl.BlockSpec((1, HEAD_DIM), lambda t: (0, 0)),
            blk(QKV_B, rd), blk(QKV_B + HEADS, rd), blk(QKV_B + 2 * HEADS, rd), blk(Z_B, rd),
        ],
        out_specs=[blk(0, wr), blk(0, rd)],
        out_shape=[out, out],
        scratch_shapes=[
            pltpu.VMEM((seq + SUBLANES, HEAD_DIM), F32),
            pltpu.VMEM((seq, HEAD_DIM), F32),
            pltpu.VMEM((seq, HEAD_DIM), F32),
            pltpu.VMEM((seq, HEAD_DIM), F32),
            pltpu.VMEM((seq, LANES), F32),
            pltpu.VMEM((seq, LANES), F32),
            pltpu.VMEM((2, n_chunks, HEAD_DIM, HEAD_DIM), BF16),
            pltpu.VMEM((2, n_chunks, HEAD_DIM, HEAD_DIM), F32),
            pltpu.VMEM((2, n_chunks, c, HEAD_DIM), BF16),
            pltpu.VMEM((2, n_chunks, c, HEAD_DIM), F32),
            pltpu.VMEM((2, n_chunks, SUBLANES, HEAD_DIM), F32),
        ],
        compiler_params=pltpu.CompilerParams(
            dimension_semantics=("arbitrary",), vmem_limit_bytes=VMEM_LIMIT),
        name="mixers",
    )(proj, proj, proj, conv_w, conv_w, conv_w, bg, proj, norm_w, proj, proj, proj, proj)


def _moba_stages(q_ref, k_ref, v_ref, zs_ref, o_ref):
    s = q_ref.shape[0]
    blk = MOBA_BLOCK
    nb = s // blk

    prow = lax.broadcasted_iota(jnp.int32, (LANES, s), 0)
    pcol = lax.broadcasted_iota(jnp.int32, (LANES, s), 1)
    pind = jnp.where(prow == pcol // blk, 1.0 / blk, 0.0).astype(BF16)
    kmean = _dot(pind, k_ref[...])
    cand = 2 * SUBLANES
    km_full = kmean.astype(BF16)
    km_hi = km_full[0:cand, :]
    km_lo = (kmean - km_full.astype(F32)).astype(BF16)[0:cand, :]

    sub = lax.broadcasted_iota(jnp.int32, (cand, blk), 0)
    key_i = lax.broadcasted_iota(jnp.int32, (blk, blk), 0)
    qry_i = lax.broadcasted_iota(jnp.int32, (blk, blk), 1)
    causal = key_i <= qry_i
    v_t = [v_ref[j * blk:(j + 1) * blk, :].astype(F32).T.astype(BF16) for j in range(nb)]

    def query_block(i):
        rows = slice(i * blk, (i + 1) * blk)
        q = q_ref[rows, :]
        pieces = [_dot_nt(k_ref[j * blk:(j + 1) * blk, :], q) for j in range(i + 1)]
        gt = _dot_nt(km_hi, q) + _dot_nt(km_lo, q) if i > MOBA_TOPK else None
        yield
        if i > MOBA_TOPK:
            rank = jnp.zeros((cand, blk), jnp.int32)
            for jp in range(i):
                gj = jnp.broadcast_to(gt[jp:jp + 1, :], (cand, blk))
                ahead = jnp.logical_or(gj > gt, jnp.logical_and(gj == gt, sub > jp))
                rank = rank + ahead.astype(jnp.int32)
            bias = jnp.where(rank >= MOBA_TOPK, NEG_INF, 0.0)
            for j in range(i):
                pieces[j] = pieces[j] + bias[j:j + 1, :]
        pieces[i] = jnp.where(causal, pieces[i], NEG_INF)
        mt = pieces[0]
        for pc in pieces[1:]:
            mt = jnp.maximum(mt, pc)
        m = jnp.max(mt, axis=0, keepdims=True)
        yield
        pt = None
        o_t = jnp.zeros((HEAD_DIM, blk), F32)
        for j, pc in enumerate(pieces):
            p = jnp.exp2(pc - m)
            pt = p if pt is None else pt + p
            o_t = o_t + _dot(v_t[j], p.astype(BF16))
            if j % 2 == 1:
                yield
        yield
        l = jnp.sum(pt, axis=0, keepdims=True)
        o = (o_t / l).T
        o_ref[rows, :] = (o * zs_ref[rows, :].astype(F32)).astype(BF16)

    order = [b for i in range(nb // 2) for b in (nb - 1 - i, i)]
    return _stagger(*[query_block(i) for i in order])


def _outproj_kernel(oa_ref, ob_ref, sa_ref, sb_ref, x_ref, wa_ref, wb_ref, wo_ref, pw_ref, o_ref):
    tm = o_ref.shape[0]
    halves = [slice(r, r + tm // OUT_SPLIT) for r in range(0, tm, tm // OUT_SPLIT)]
    ua = [_dot(oa_ref[r, :], wa_ref[...]) for r in halves]
    ub = [_dot(ob_ref[r, :], wb_ref[...]) for r in halves]
    merged = [(sa_ref[r, :].astype(F32) * a + sb_ref[r, :].astype(F32) * b).astype(BF16)
              for r, a, b in zip(halves, ua, ub)]
    ys = [_dot(mg, wo_ref[...]) for mg in merged]
    for r, y in zip(halves, ys):
        inv = lax.rsqrt(jnp.mean(y * y, axis=-1, keepdims=True) + NORM_EPS)
        o_ref[r, :] = x_ref[r, :] + y * inv * pw_ref[...]


def _outproj(oa, ob, proj, x2, wa, wb, wo, pw):
    m, d = x2.shape
    width = oa.shape[1]
    tm = OUT_TM
    const = lambda shape: pl.BlockSpec(shape, lambda i: (0, 0), pipeline_mode=pl.Buffered(1))
    return pl.pallas_call(
        _outproj_kernel,
        grid=(m // tm,),
        in_specs=[
            pl.BlockSpec((tm, width), lambda i: (i, 0)),
            pl.BlockSpec((tm, width), lambda i: (i, 0)),
            pl.BlockSpec((tm, d), lambda i: (i, GATE_A_BLK2048)),
            pl.BlockSpec((tm, d), lambda i: (i, GATE_B_BLK2048)),
            pl.BlockSpec((tm, d), lambda i: (i, 0)),
            const((width, d)), const((width, d)), const((d, d)), const((1, d)),
        ],
        out_specs=pl.BlockSpec((tm, d), lambda i: (i, 0)),
        out_shape=jax.ShapeDtypeStruct((m, d), F32),
        compiler_params=pltpu.CompilerParams(
            dimension_semantics=("parallel",), vmem_limit_bytes=VMEM_LIMIT),
        name="outproj",
    )(oa, ob, proj, proj, x2, wa, wb, wo, pw)


def _layer(x, pre_w, w_in, conv_w, a_log, dt_bias, gdn_norm_w, w_a, w_b, w_out, post_w):
    batch, seq, d = x.shape
    width = HEADS * HEAD_DIM
    ba_lo = 4 * width
    ba_hi = ba_lo + 2 * HEADS
    x2 = x.reshape(batch * seq, d)
    w_t = jnp.swapaxes(w_in, 0, 1)
    w_ba = jnp.pad(w_t[ba_lo:ba_hi, :], ((0, LANES - 2 * HEADS), (0, 0)))
    pad_row = lambda v: jnp.pad(v.astype(F32), (HEADS, LANES - 2 * HEADS)).reshape(1, LANES)
    proj, bg = _inproj(x2, pre_w.reshape(1, d), w_t, w_ba, pad_row(a_log), pad_row(dt_bias))
    oa, ob = _mixers(proj, conv_w, bg, gdn_norm_w.reshape(1, HEAD_DIM), batch, seq)
    out = _outproj(oa, ob, proj, x2, w_a.astype(BF16), w_b.astype(BF16), w_out.astype(BF16),
                   post_w.reshape(1, d))
    return out.reshape(batch, seq, d)


def kernel(x, pre_norm_w, w_in, conv_w, a_log, dt_bias, gdn_norm_w, w_branch_a, w_branch_b, w_out,
           post_norm_w):
    depth = pre_norm_w.shape[0]
    for l in range(depth):
        x = _layer(x, pre_norm_w[l], w_in[l], conv_w[l], a_log[l], dt_bias[l], gdn_norm_w[l],
                   w_branch_a[l], w_branch_b[l], w_out[l], post_norm_w[l])
    return x
```

```python
import jax
import jax.numpy as jnp
from jax import lax
from jax.experimental import pallas as pl
from jax.experimental.pallas import tpu as pltpu

F32 = jnp.float32
BF16 = jnp.bfloat16

NORM_EPS = 1e-6
NEG_INF = -1e30

HEADS = 8
HEAD_DIM = 128
CONV_WIDTH = 4
MOBA_BLOCK = 256
MOBA_TOPK = 3
GDN_CHUNK = 128
GDN_GROUP = 16
STAGGER_PERIOD = 1

LANES = 128
SUBLANES = 8

QKV_A, Z_A, QKV_B, Z_B = 0, 24, 32, 56
GATE_A_BLK2048, GATE_B_BLK2048 = 4, 5

INPROJ_TM = 1024
INPROJ_TN = 1024
TILE_BLKS = INPROJ_TN // LANES
N_LO_TILES = (4 * HEADS * HEAD_DIM) // INPROJ_TN
OUT_TM = 512
OUT_SPLIT = 4

MOBA_QSCALE = (HEAD_DIM ** -0.5) * 1.4426950408889634

VMEM_LIMIT = 56 * 1024 * 1024


def _dot(a, b):
    return jnp.dot(a, b, preferred_element_type=F32)


def _dot_nt(a, b):
    return lax.dot_general(a, b, (((1,), (1,)), ((), ())), preferred_element_type=F32)


def _dot_tn(a, b):
    return lax.dot_general(a, b, (((0,), (0,)), ((), ())), preferred_element_type=F32)


def _sigmoid(x):
    return 0.5 * jnp.tanh(0.5 * x) + 0.5


def _softplus(x):
    return jnp.maximum(x, 0.0) + jnp.log1p(jnp.exp(-jnp.abs(x)))


def _inproj_kernel(x_ref, nw_ref, w_ref, wba_ref, alog_ref, dtb_ref, proj_ref, bg_ref, h_ref):
    j = pl.program_id(1)

    @pl.when(j == 0)
    def _():
        xf = x_ref[...]
        inv = lax.rsqrt(jnp.mean(xf * xf, axis=-1, keepdims=True) + NORM_EPS)
        h_ref[...] = (xf * inv * nw_ref[...]).astype(h_ref.dtype)
        ba = _dot_nt(h_ref[...], wba_ref[...])
        lane = lax.broadcasted_iota(jnp.int32, ba.shape, 1)
        beta = _sigmoid(ba)
        g = -jnp.exp(alog_ref[...]) * _softplus(ba + dtb_ref[...])
        bg_ref[...] = jnp.where(lane < HEADS, beta, g)

    is_silu = jnp.logical_or(j == Z_A // TILE_BLKS, j == Z_B // TILE_BLKS)
    lin = jnp.where(j == QKV_B // TILE_BLKS, MOBA_QSCALE, 1.0)

    acc = _dot_nt(h_ref[...], w_ref[...])

    @pl.when(jnp.logical_not(is_silu))
    def _():
        proj_ref[...] = (acc * lin).astype(BF16)

    @pl.when(is_silu)
    def _():
        half = 0.5 * acc
        proj_ref[...] = (half * (1.0 + jnp.tanh(half))).astype(BF16)


def _inproj(x2, nw, w_t, w_ba, alog_row, dtb_row):
    m, d = x2.shape
    tm, tn = INPROJ_TM, INPROJ_TN
    n = w_t.shape[0] - 2 * HEADS
    assert n % tn == 0 and N_LO_TILES * tn == 4 * HEADS * HEAD_DIM

    def w_rows(i, j):
        return (pl.multiple_of(j * tn + jnp.where(j >= N_LO_TILES, 2 * HEADS, 0), 2 * HEADS), 0)

    return pl.pallas_call(
        _inproj_kernel,
        grid=(m // tm, n // tn),
        in_specs=[
            pl.BlockSpec((tm, d), lambda i, j: (i, 0)),
            pl.BlockSpec((1, d), lambda i, j: (0, 0)),
            pl.BlockSpec((pl.Element(tn), pl.Element(d)), w_rows),
            pl.BlockSpec((LANES, d), lambda i, j: (0, 0)),
            pl.BlockSpec((1, LANES), lambda i, j: (0, 0)),
            pl.BlockSpec((1, LANES), lambda i, j: (0, 0)),
        ],
        out_specs=[
            pl.BlockSpec((tm, tn), lambda i, j: (i, j)),
            pl.BlockSpec((tm, LANES), lambda i, j: (i, 0)),
        ],
        out_shape=[
            jax.ShapeDtypeStruct((m, n), BF16),
            jax.ShapeDtypeStruct((m, LANES), F32),
        ],
        scratch_shapes=[pltpu.VMEM((tm, d), BF16)],
        compiler_params=pltpu.CompilerParams(
            dimension_semantics=("parallel", "arbitrary"), vmem_limit_bytes=VMEM_LIMIT),
        name="inproj",
    )(x2, nw, w_t, w_ba, alog_row, dtb_row)


def _interleave(*stages):
    live = list(stages)
    while live:
        for g in list(live):
            try:
                next(g)
            except StopIteration:
                live.remove(g)


def _stagger(*stages):
    live = []
    pending = list(stages)
    rnd = 0
    while live or pending:
        if pending and rnd % STAGGER_PERIOD == 0:
            live.append(pending.pop(0))
        rnd += 1
        for g in list(live):
            try:
                next(g)
            except StopIteration:
                live.remove(g)
        yield


def _tri_inverse_group(lmats, row, col, out):
    c = lmats[0].shape[0]

    def sub_mask(s_log):
        bi = jnp.right_shift(row, s_log)
        bj = jnp.right_shift(col, s_log)
        return jnp.logical_and(jnp.bitwise_xor(bi, bj) == 1, jnp.bitwise_and(bi, 1) == 1)

    eye = (row == col).astype(BF16)
    lbs = [lm.astype(BF16) for lm in lmats]
    m0 = sub_mask(0).astype(BF16)
    tds = [eye - lb * m0 for lb in lbs]
    s_log = 1
    while (1 << s_log) < c:
        ms = sub_mask(s_log).astype(BF16)
        x1s = [_dot(td, lb * ms).astype(BF16) for td, lb in zip(tds, lbs)]
        yield
        tds = [td - _dot(x1, td).astype(BF16) for td, x1 in zip(tds, x1s)]
        yield
        s_log += 1
    out.extend(tds)


def _mixers_kernel(q_ref, k_ref, v_ref, cwq_ref, cwk_ref, cwv_ref, bg_ref, zs_ref, nw_ref,
                   mq_ref, mk_ref, mv_ref, mzs_ref, o_ref, mo_ref,
                   xpad, qn, kn, vn, betab, gb, xs, cs, qps, olocs, gls):
    t = pl.program_id(0)

    @pl.when(t == 0)
    def _():
        xs[1] = jnp.zeros(xs.shape[1:], xs.dtype)
        cs[1] = jnp.zeros(cs.shape[1:], cs.dtype)
        qps[1] = jnp.zeros(qps.shape[1:], qps.dtype)
        olocs[1] = jnp.zeros(olocs.shape[1:], olocs.dtype)
        gls[1] = jnp.zeros(gls.shape[1:], gls.dtype)

    gdn_stages = _gdn_stages(q_ref, k_ref, v_ref, cwq_ref, cwk_ref, cwv_ref, bg_ref, zs_ref, nw_ref,
                             o_ref, xpad, qn, kn, vn, betab, gb, xs, cs, qps, olocs, gls)
    moba_stage = _moba_stages(mq_ref, mk_ref, mv_ref, mzs_ref, mo_ref)
    _interleave(*gdn_stages, moba_stage)


def _gdn_stages(q_ref, k_ref, v_ref, cwq_ref, cwk_ref, cwv_ref, bg_ref, zs_ref, nw_ref, o_ref,
                xpad, qn, kn, vn, betab, gb, xs, cs, qps, olocs, gls):
    s = q_ref.shape[0]
    c = GDN_CHUNK
    n_chunks = s // c
    t = pl.program_id(0)
    n_heads_total = pl.num_programs(0) - 1
    h = lax.rem(jnp.minimum(t, n_heads_total - 1), HEADS)
    cur = lax.rem(t, 2)
    prv = 1 - cur

    def conv_silu(src_ref, cw_ref):
        xpad[0:SUBLANES, :] = jnp.zeros((SUBLANES, HEAD_DIM), F32)
        xpad[SUBLANES:SUBLANES + s, :] = src_ref[...].astype(F32)
        base = SUBLANES - (CONV_WIDTH - 1)
        acc = cw_ref[0:1, :] * xpad[base:base + s, :]
        for j in range(1, CONV_WIDTH):
            acc = acc + cw_ref[j:j + 1, :] * xpad[base + j:base + j + s, :]
        half = 0.5 * acc
        return half * (1.0 + jnp.tanh(half))

    def l2n(x, scale=1.0):
        return x * (lax.rsqrt(jnp.sum(x * x, axis=-1, keepdims=True) + NORM_EPS) * scale)

    def conv_steps():
        qn[...] = l2n(conv_silu(q_ref, cwq_ref), HEAD_DIM ** -0.5)
        yield
        kn[...] = l2n(conv_silu(k_ref, cwk_ref))
        yield
        vn[...] = conv_silu(v_ref, cwv_ref)
        yield
        bg = bg_ref[...]
        lane = lax.broadcasted_iota(jnp.int32, bg.shape, 1)
        beta_col = jnp.sum(jnp.where(lane == h, bg, 0.0), axis=-1, keepdims=True)
        g_col = jnp.sum(jnp.where(lane == h + HEADS, bg, 0.0), axis=-1, keepdims=True)
        betab[...] = jnp.broadcast_to(beta_col, bg.shape)
        gb[...] = jnp.broadcast_to(g_col, bg.shape)
        yield

    row = lax.broadcasted_iota(jnp.int32, (c, c), 0)
    col = lax.broadcasted_iota(jnp.int32, (c, c), 1)
    incl = row >= col
    strict = row > col
    tril_b = incl.astype(BF16)

    def cumsum_rows(x):
        x1 = x.astype(BF16)
        r1 = x - x1.astype(F32)
        x2 = r1.astype(BF16)
        x3 = (r1 - x2.astype(F32)).astype(BF16)
        return _dot(tril_b, x1) + _dot(tril_b, x2) + _dot(tril_b, x3)

    def prep_stage(ns):
        rows = [slice(n * c, (n + 1) * c) for n in ns]
        lane = lax.broadcasted_iota(jnp.int32, (c, LANES), 1)
        gcol = jnp.zeros((c, LANES), F32)
        for n, r in zip(ns, rows):
            gcol = jnp.where(lane == n, gb[r, :], gcol)
        csum = cumsum_rows(gcol)
        csum_t = csum.T
        yield
        gcs = [jnp.broadcast_to(csum[:, n:n + 1], (c, c)) for n in ns]
        decays = [jnp.exp(jnp.where(incl, gc - jnp.broadcast_to(csum_t[n:n + 1, :], (c, c)),
                                    NEG_INF)) for n, gc in zip(ns, gcs)]
        a2s = [_dot_nt(jnp.concatenate([kn[r, :] * betab[r, :], qn[r, :]], axis=0).astype(BF16),
                       kn[r, :].astype(BF16)) for r in rows]
        yield
        lmats = [jnp.where(strict, a2[:c] * d, 0.0) for a2, d in zip(a2s, decays)]
        ts = []
        yield from _tri_inverse_group(lmats, row, col, ts)
        uwbs = []
        for r, gc, tinv in zip(rows, gcs, ts):
            beta = betab[r, :]
            rhs = jnp.concatenate([kn[r, :] * beta * jnp.exp(gc), vn[r, :] * beta], axis=1)
            uwbs.append(_dot(tinv.astype(BF16), rhs.astype(BF16)).astype(BF16))
        yield
        for n, r, gc, a2, d, uwb in zip(ns, rows, gcs, a2s, decays, uwbs):
            g_last = gc[c - 1:c, :]
            intra = jnp.where(incl, a2[c:] * d, 0.0)
            kdec = (kn[r, :] * jnp.exp(g_last - gc)).astype(BF16)
            kwu = _dot_tn(kdec, uwb)
            iwu = _dot(intra.astype(BF16), uwb)
            xs[cur, n] = kwu[:, :HEAD_DIM].astype(BF16)
            cs[cur, n] = kwu[:, HEAD_DIM:]
            qps[cur, n] = (qn[r, :] * jnp.exp(gc) - iwu[:, :HEAD_DIM]).astype(BF16)
            olocs[cur, n] = iwu[:, HEAD_DIM:]
            gls[cur, n] = jnp.broadcast_to(jnp.exp(g_last), (SUBLANES, HEAD_DIM))
            if n % 4 == 3:
                yield

    def recurrence_stage():
        nw = nw_ref[...]
        state = jnp.zeros((HEAD_DIM, HEAD_DIM), F32)
        for n in range(n_chunks):
            rows = slice(n * c, (n + 1) * c)
            sb = state.astype(BF16)
            o = _dot(qps[prv, n], sb) + olocs[prv, n]
            o = o * lax.rsqrt(jnp.mean(o * o, axis=-1, keepdims=True) + NORM_EPS)
            o = o * nw * zs_ref[rows, :].astype(F32)
            o_ref[rows, :] = o.astype(BF16)
            state = state * gls[prv, n][0:1, :] + cs[prv, n] - _dot(xs[prv, n], sb)
            yield

    for _ in conv_steps():
        pass

    def prep_groups():
        for g0 in range(0, n_chunks, GDN_GROUP):
            yield from prep_stage(list(range(g0, g0 + GDN_GROUP)))

    return prep_groups(), recurrence_stage()


def _mixers(proj, conv_w, bg, norm_w, batch, seq):
    c = GDN_CHUNK
    n_chunks = seq // c
    n_heads_total = batch * HEADS
    rd = lambda t: jnp.minimum(t, n_heads_total - 1)
    wr = lambda t: jnp.maximum(t - 1, 0)
    blk = lambda off, hd: pl.BlockSpec(
        (seq, HEAD_DIM), lambda t: (hd(t) // HEADS, off + lax.rem(hd(t), HEADS)))
    cw = lambda off: pl.BlockSpec(
        (CONV_WIDTH, HEAD_DIM), lambda t: (0, off + lax.rem(rd(t), HEADS)))
    out = jax.ShapeDtypeStruct((batch * seq, HEADS * HEAD_DIM), BF16)
    return pl.pallas_call(
        _mixers_kernel,
        grid=(n_heads_total + 1,),
        in_specs=[
            blk(QKV_A, rd), blk(QKV_A + HEADS, rd), blk(QKV_A + 2 * HEADS, rd),
            cw(0), cw(HEADS), cw(2 * HEADS),
            pl.BlockSpec((seq, LANES), lambda t: (rd(t) // HEADS, 0)),
            blk(Z_A, wr),
            pl.BlockSpec((1, HEAD_DIM), lambda t: (0, 0)),
            blk(QKV_B, rd), blk(QKV_B + HEADS, rd), blk(QKV_B + 2 * HEADS, rd), blk(Z_B, rd),
        ],
        out_specs=[blk(0, wr), blk(0, rd)],
        out_shape=[out, out],
        scratch_shapes=[
            pltpu.VMEM((seq + SUBLANES, HEAD_DIM), F32),
            pltpu.VMEM((seq, HEAD_DIM), F32),
            pltpu.VMEM((seq, HEAD_DIM), F32),
            pltpu.VMEM((seq, HEAD_DIM), F32),
            pltpu.VMEM((seq, LANES), F32),
            pltpu.VMEM((seq, LANES), F32),
            pltpu.VMEM((2, n_chunks, HEAD_DIM, HEAD_DIM), BF16),
            pltpu.VMEM((2, n_chunks, HEAD_DIM, HEAD_DIM), F32),
            pltpu.VMEM((2, n_chunks, c, HEAD_DIM), BF16),
            pltpu.VMEM((2, n_chunks, c, HEAD_DIM), F32),
            pltpu.VMEM((2, n_chunks, SUBLANES, HEAD_DIM), F32),
        ],
        compiler_params=pltpu.CompilerParams(
            dimension_semantics=("arbitrary",), vmem_limit_bytes=VMEM_LIMIT),
        name="mixers",
    )(proj, proj, proj, conv_w, conv_w, conv_w, bg, proj, norm_w, proj, proj, proj, proj)


def _moba_stages(q_ref, k_ref, v_ref, zs_ref, o_ref):
    s = q_ref.shape[0]
    blk = MOBA_BLOCK
    nb = s // blk

    prow = lax.broadcasted_iota(jnp.int32, (LANES, s), 0)
    pcol = lax.broadcasted_iota(jnp.int32, (LANES, s), 1)
    pind = jnp.where(prow == pcol // blk, 1.0 / blk, 0.0).astype(BF16)
    kmean = _dot(pind, k_ref[...])
    cand = 2 * SUBLANES
    km_full = kmean.astype(BF16)
    km_hi = km_full[0:cand, :]
    km_lo = (kmean - km_full.astype(F32)).astype(BF16)[0:cand, :]

    sub = lax.broadcasted_iota(jnp.int32, (cand, blk), 0)
    key_i = lax.broadcasted_iota(jnp.int32, (blk, blk), 0)
    qry_i = lax.broadcasted_iota(jnp.int32, (blk, blk), 1)
    causal = key_i <= qry_i
    v_t = [v_ref[j * blk:(j + 1) * blk, :].astype(F32).T.astype(BF16) for j in range(nb)]

    def query_block(i):
        rows = slice(i * blk, (i + 1) * blk)
        q = q_ref[rows, :]
        pieces = [_dot_nt(k_ref[j * blk:(j + 1) * blk, :], q) for j in range(i + 1)]
        gt = _dot_nt(km_hi, q) + _dot_nt(km_lo, q) if i > MOBA_TOPK else None
        yield
        if i > MOBA_TOPK:
            rank = jnp.zeros((cand, blk), jnp.int32)
            for jp in range(i):
                gj = jnp.broadcast_to(gt[jp:jp + 1, :], (cand, blk))
                ahead = jnp.logical_or(gj > gt, jnp.logical_and(gj == gt, sub > jp))
                rank = rank + ahead.astype(jnp.int32)
            bias = jnp.where(rank >= MOBA_TOPK, NEG_INF, 0.0)
            for j in range(i):
                pieces[j] = pieces[j] + bias[j:j + 1, :]
        pieces[i] = jnp.where(causal, pieces[i], NEG_INF)
        mt = pieces[0]
        for pc in pieces[1:]:
            mt = jnp.maximum(mt, pc)
        m = jnp.max(mt, axis=0, keepdims=True)
        yield
        pt = None
        o_t = jnp.zeros((HEAD_DIM, blk), F32)
        for j, pc in enumerate(pieces):
            p = jnp.exp2(pc - m)
            pt = p if pt is None else pt + p
            o_t = o_t + _dot(v_t[j], p.astype(BF16))
            if j % 2 == 1:
                yield
        yield
        l = jnp.sum(pt, axis=0, keepdims=True)
        o = (o_t / l).T
        o_ref[rows, :] = (o * zs_ref[rows, :].astype(F32)).astype(BF16)

    order = [b for i in range(nb // 2) for b in (nb - 1 - i, i)]
    return _stagger(*[query_block(i) for i in order])


def _outproj_kernel(oa_ref, ob_ref, ga_ref, gb_ref, x_ref, wa_ref, wb_ref, wo_ref, pw_ref, o_ref):
    tm = o_ref.shape[0]
    halves = [slice(r, r + tm // OUT_SPLIT) for r in range(0, tm, tm // OUT_SPLIT)]
    ua = [_dot(oa_ref[r, :], wa_ref[...]) for r in halves]
    ub = [_dot(ob_ref[r, :], wb_ref[...]) for r in halves]
    merged = [(_sigmoid(ga_ref[r, :].astype(F32)) * a
               + _sigmoid(gb_ref[r, :].astype(F32)) * b).astype(BF16)
              for r, a, b in zip(halves, ua, ub)]
    ys = [_dot(mg, wo_ref[...]) for mg in merged]
    for r, y in zip(halves, ys):
        inv = lax.rsqrt(jnp.mean(y * y, axis=-1, keepdims=True) + NORM_EPS)
        o_ref[r, :] = x_ref[r, :] + y * inv * pw_ref[...]


def _outproj(oa, ob, proj, x2, wa, wb, wo, pw):
    m, d = x2.shape
    width = oa.shape[1]
    tm = OUT_TM
    const = lambda shape: pl.BlockSpec(shape, lambda i: (0, 0), pipeline_mode=pl.Buffered(1))
    return pl.pallas_call(
        _outproj_kernel,
        grid=(m // tm,),
        in_specs=[
            pl.BlockSpec((tm, width), lambda i: (i, 0)),
            pl.BlockSpec((tm, width), lambda i: (i, 0)),
            pl.BlockSpec((tm, d), lambda i: (i, GATE_A_BLK2048)),
            pl.BlockSpec((tm, d), lambda i: (i, GATE_B_BLK2048)),
            pl.BlockSpec((tm, d), lambda i: (i, 0)),
            const((width, d)), const((width, d)), const((d, d)), const((1, d)),
        ],
        out_specs=pl.BlockSpec((tm, d), lambda i: (i, 0)),
        out_shape=jax.ShapeDtypeStruct((m, d), F32),
        compiler_params=pltpu.CompilerParams(
            dimension_semantics=("parallel",), vmem_limit_bytes=VMEM_LIMIT),
        name="outproj",
    )(oa, ob, proj, proj, x2, wa, wb, wo, pw)


def _layer(x, pre_w, w_in, conv_w, a_log, dt_bias, gdn_norm_w, w_a, w_b, w_out, post_w):
    batch, seq, d = x.shape
    width = HEADS * HEAD_DIM
    ba_lo = 4 * width
    ba_hi = ba_lo + 2 * HEADS
    x2 = x.reshape(batch * seq, d)
    w_t = jnp.swapaxes(w_in, 0, 1)
    w_ba = jnp.pad(w_t[ba_lo:ba_hi, :], ((0, LANES - 2 * HEADS), (0, 0)))
    pad_row = lambda v: jnp.pad(v.astype(F32), (HEADS, LANES - 2 * HEADS)).reshape(1, LANES)
    proj, bg = _inproj(x2, pre_w.reshape(1, d), w_t, w_ba, pad_row(a_log), pad_row(dt_bias))
    oa, ob = _mixers(proj, conv_w, bg, gdn_norm_w.reshape(1, HEAD_DIM), batch, seq)
    out = _outproj(oa, ob, proj, x2, w_a.astype(BF16), w_b.astype(BF16), w_out.astype(BF16),
                   post_w.reshape(1, d))
    return out.reshape(batch, seq, d)


def kernel(x, pre_norm_w, w_in, conv_w, a_log, dt_bias, gdn_norm_w, w_branch_a, w_branch_b, w_out,
           post_norm_w):
    depth = pre_norm_w.shape[0]
    for l in range(depth):
        x = _layer(x, pre_norm_w[l], w_in[l], conv_w[l], a_log[l], dt_bias[l], gdn_norm_w[l],
                   w_branch_a[l], w_branch_b[l], w_out[l], post_norm_w[l])
    return x
```

```python
import jax
import jax.numpy as jnp
from jax import lax
from jax.experimental import pallas as pl
from jax.experimental.pallas import tpu as pltpu

F32 = jnp.float32
BF16 = jnp.bfloat16

NORM_EPS = 1e-6
NEG_INF = -1e30

HEADS = 8
HEAD_DIM = 128
CONV_WIDTH = 4
MOBA_BLOCK = 256
MOBA_TOPK = 3
GDN_CHUNK = 128
GDN_GROUP = 16
STAGGER_PERIOD = 1

LANES = 128
SUBLANES = 8

QKV_A, Z_A, QKV_B, Z_B = 0, 24, 32, 56
GATE_A_BLK2048, GATE_B_BLK2048 = 4, 5

INPROJ_TM = 1024
INPROJ_TN = 1024
TILE_BLKS = INPROJ_TN // LANES
N_LO_TILES = (4 * HEADS * HEAD_DIM) // INPROJ_TN
OUT_TM = 512
OUT_SPLIT = 4

MOBA_QSCALE = (HEAD_DIM ** -0.5) * 1.4426950408889634

VMEM_LIMIT = 56 * 1024 * 1024


def _dot(a, b):
    return jnp.dot(a, b, preferred_element_type=F32)


def _dot_nt(a, b):
    return lax.dot_general(a, b, (((1,), (1,)), ((), ())), preferred_element_type=F32)


def _dot_tn(a, b):
    return lax.dot_general(a, b, (((0,), (0,)), ((), ())), preferred_element_type=F32)


def _sigmoid(x):
    return 0.5 * jnp.tanh(0.5 * x) + 0.5


def _softplus(x):
    return jnp.maximum(x, 0.0) + jnp.log1p(jnp.exp(-jnp.abs(x)))


def _inproj_kernel(x_ref, nw_ref, w_ref, wba_ref, alog_ref, dtb_ref, proj_ref, bg_ref, h_ref):
    j = pl.program_id(1)

    @pl.when(j == 0)
    def _():
        xf = x_ref[...]
        inv = lax.rsqrt(jnp.mean(xf * xf, axis=-1, keepdims=True) + NORM_EPS)
        h_ref[...] = (xf * inv * nw_ref[...]).astype(h_ref.dtype)
        ba = _dot_nt(h_ref[...], wba_ref[...])
        lane = lax.broadcasted_iota(jnp.int32, ba.shape, 1)
        beta = _sigmoid(ba)
        g = -jnp.exp(alog_ref[...]) * _softplus(ba + dtb_ref[...])
        bg_ref[...] = jnp.where(lane < HEADS, beta, g)

    is_silu = jnp.logical_or(j == Z_A // TILE_BLKS, j == Z_B // TILE_BLKS)
    lin = jnp.where(j == QKV_B // TILE_BLKS, MOBA_QSCALE, 1.0)

    proj_ref[...] = (_dot_nt(h_ref[...], w_ref[...]) * lin).astype(BF16)

    @pl.when(is_silu)
    def _():
        half = 0.5 * proj_ref[...].astype(F32)
        proj_ref[...] = (half * (1.0 + jnp.tanh(half))).astype(BF16)


def _inproj(x2, nw, w_t, w_ba, alog_row, dtb_row):
    m, d = x2.shape
    tm, tn = INPROJ_TM, INPROJ_TN
    n = w_t.shape[0] - 2 * HEADS
    assert n % tn == 0 and N_LO_TILES * tn == 4 * HEADS * HEAD_DIM

    def w_rows(i, j):
        return (pl.multiple_of(j * tn + jnp.where(j >= N_LO_TILES, 2 * HEADS, 0), 2 * HEADS), 0)

    return pl.pallas_call(
        _inproj_kernel,
        grid=(m // tm, n // tn),
        in_specs=[
            pl.BlockSpec((tm, d), lambda i, j: (i, 0)),
            pl.BlockSpec((1, d), lambda i, j: (0, 0)),
            pl.BlockSpec((pl.Element(tn), pl.Element(d)), w_rows),
            pl.BlockSpec((LANES, d), lambda i, j: (0, 0)),
            pl.BlockSpec((1, LANES), lambda i, j: (0, 0)),
            pl.BlockSpec((1, LANES), lambda i, j: (0, 0)),
        ],
        out_specs=[
            pl.BlockSpec((tm, tn), lambda i, j: (i, j)),
            pl.BlockSpec((tm, LANES), lambda i, j: (i, 0)),
        ],
        out_shape=[
            jax.ShapeDtypeStruct((m, n), BF16),
            jax.ShapeDtypeStruct((m, LANES), F32),
        ],
        scratch_shapes=[pltpu.VMEM((tm, d), BF16)],
        compiler_params=pltpu.CompilerParams(
            dimension_semantics=("parallel", "arbitrary"), vmem_limit_bytes=VMEM_LIMIT),
        name="inproj",
    )(x2, nw, w_t, w_ba, alog_row, dtb_row)


def _interleave(*stages):
    live = list(stages)
    while live:
        for g in list(live):
            try:
                next(g)
            except StopIteration:
                live.remove(g)


def _stagger(*stages):
    live = []
    pending = list(stages)
    rnd = 0
    while live or pending:
        if pending and rnd % STAGGER_PERIOD == 0:
            live.append(pending.pop(0))
        rnd += 1
        for g in list(live):
            try:
                next(g)
            except StopIteration:
                live.remove(g)
        yield


def _tri_inverse_group(lmats, row, col, out):
    c = lmats[0].shape[0]

    def sub_mask(s_log):
        bi = jnp.right_shift(row, s_log)
        bj = jnp.right_shift(col, s_log)
        return jnp.logical_and(jnp.bitwise_xor(bi, bj) == 1, jnp.bitwise_and(bi, 1) == 1)

    eye = (row == col).astype(BF16)
    lbs = [lm.astype(BF16) for lm in lmats]
    m0 = sub_mask(0).astype(BF16)
    tds = [eye - lb * m0 for lb in lbs]
    s_log = 1
    while (1 << s_log) < c:
        ms = sub_mask(s_log).astype(BF16)
        x1s = [_dot(td, lb * ms).astype(BF16) for td, lb in zip(tds, lbs)]
        yield
        tds = [td - _dot(x1, td).astype(BF16) for td, x1 in zip(tds, x1s)]
        yield
        s_log += 1
    out.extend(tds)


def _mixers_kernel(q_ref, k_ref, v_ref, cwq_ref, cwk_ref, cwv_ref, bg_ref, zs_ref, nw_ref,
                   mq_ref, mk_ref, mv_ref, mzs_ref, o_ref, mo_ref,
                   xpad, qn, kn, vn, betab, gb, xs, cs, qps, olocs, gls):
    t = pl.program_id(0)

    @pl.when(t == 0)
    def _():
        xs[1] = jnp.zeros(xs.shape[1:], xs.dtype)
        cs[1] = jnp.zeros(cs.shape[1:], cs.dtype)
        qps[1] = jnp.zeros(qps.shape[1:], qps.dtype)
        olocs[1] = jnp.zeros(olocs.shape[1:], olocs.dtype)
        gls[1] = jnp.zeros(gls.shape[1:], gls.dtype)

    gdn_stages = _gdn_stages(q_ref, k_ref, v_ref, cwq_ref, cwk_ref, cwv_ref, bg_ref, zs_ref, nw_ref,
                             o_ref, xpad, qn, kn, vn, betab, gb, xs, cs, qps, olocs, gls)
    moba_stage = _moba_stages(mq_ref, mk_ref, mv_ref, mzs_ref, mo_ref)
    _interleave(*gdn_stages, moba_stage)


def _gdn_stages(q_ref, k_ref, v_ref, cwq_ref, cwk_ref, cwv_ref, bg_ref, zs_ref, nw_ref, o_ref,
                xpad, qn, kn, vn, betab, gb, xs, cs, qps, olocs, gls):
    s = q_ref.shape[0]
    c = GDN_CHUNK
    n_chunks = s // c
    t = pl.program_id(0)
    n_heads_total = pl.num_programs(0) - 1
    h = lax.rem(jnp.minimum(t, n_heads_total - 1), HEADS)
    cur = lax.rem(t, 2)
    prv = 1 - cur

    def conv_silu(src_ref, cw_ref):
        xpad[0:SUBLANES, :] = jnp.zeros((SUBLANES, HEAD_DIM), F32)
        xpad[SUBLANES:SUBLANES + s, :] = src_ref[...].astype(F32)
        base = SUBLANES - (CONV_WIDTH - 1)
        acc = cw_ref[0:1, :] * xpad[base:base + s, :]
        for j in range(1, CONV_WIDTH):
            acc = acc + cw_ref[j:j + 1, :] * xpad[base + j:base + j + s, :]
        half = 0.5 * acc
        return half * (1.0 + jnp.tanh(half))

    def l2n(x, scale=1.0):
        return x * (lax.rsqrt(jnp.sum(x * x, axis=-1, keepdims=True) + NORM_EPS) * scale)

    def conv_steps():
        qn[...] = l2n(conv_silu(q_ref, cwq_ref), HEAD_DIM ** -0.5)
        yield
        kn[...] = l2n(conv_silu(k_ref, cwk_ref))
        yield
        vn[...] = conv_silu(v_ref, cwv_ref)
        yield
        bg = bg_ref[...]
        lane = lax.broadcasted_iota(jnp.int32, bg.shape, 1)
        beta_col = jnp.sum(jnp.where(lane == h, bg, 0.0), axis=-1, keepdims=True)
        g_col = jnp.sum(jnp.where(lane == h + HEADS, bg, 0.0), axis=-1, keepdims=True)
        betab[...] = jnp.broadcast_to(beta_col, bg.shape)
        gb[...] = jnp.broadcast_to(g_col, bg.shape)
        yield

    row = lax.broadcasted_iota(jnp.int32, (c, c), 0)
    col = lax.broadcasted_iota(jnp.int32, (c, c), 1)
    incl = row >= col
    strict = row > col
    tril_b = incl.astype(BF16)

    def cumsum_rows(x):
        x1 = x.astype(BF16)
        r1 = x - x1.astype(F32)
        x2 = r1.astype(BF16)
        x3 = (r1 - x2.astype(F32)).astype(BF16)
        return _dot(tril_b, x1) + _dot(tril_b, x2) + _dot(tril_b, x3)

    def prep_stage(ns):
        rows = [slice(n * c, (n + 1) * c) for n in ns]
        lane = lax.broadcasted_iota(jnp.int32, (c, LANES), 1)
        gcol = jnp.zeros((c, LANES), F32)
        for n, r in zip(ns, rows):
            gcol = jnp.where(lane == n, gb[r, :], gcol)
        csum = cumsum_rows(gcol)
        csum_t = csum.T
        yield
        gcs = [jnp.broadcast_to(csum[:, n:n + 1], (c, c)) for n in ns]
        decays = [jnp.exp(jnp.where(incl, gc - jnp.broadcast_to(csum_t[n:n + 1, :], (c, c)),
                                    NEG_INF)) for n, gc in zip(ns, gcs)]
        a2s = [_dot_nt(jnp.concatenate([kn[r, :] * betab[r, :], qn[r, :]], axis=0).astype(BF16),
                       kn[r, :].astype(BF16)) for r in rows]
        yield
        lmats = [jnp.where(strict, a2[:c] * d, 0.0) for a2, d in zip(a2s, decays)]
        ts = []
        yield from _tri_inverse_group(lmats, row, col, ts)
        uwbs = []
        for r, gc, tinv in zip(rows, gcs, ts):
            beta = betab[r, :]
            rhs = jnp.concatenate([kn[r, :] * beta * jnp.exp(gc), vn[r, :] * beta], axis=1)
            uwbs.append(_dot(tinv.astype(BF16), rhs.astype(BF16)).astype(BF16))
        yield
        for n, r, gc, a2, d, uwb in zip(ns, rows, gcs, a2s, decays, uwbs):
            g_last = gc[c - 1:c, :]
            intra = jnp.where(incl, a2[c:] * d, 0.0)
            kdec = (kn[r, :] * jnp.exp(g_last - gc)).astype(BF16)
            kwu = _dot_tn(kdec, uwb)
            iwu = _dot(intra.astype(BF16), uwb)
            xs[cur, n] = kwu[:, :HEAD_DIM].astype(BF16)
            cs[cur, n] = kwu[:, HEAD_DIM:]
            qps[cur, n] = (qn[r, :] * jnp.exp(gc) - iwu[:, :HEAD_DIM]).astype(BF16)
            olocs[cur, n] = iwu[:, HEAD_DIM:]
            gls[cur, n] = jnp.broadcast_to(jnp.exp(g_last), (SUBLANES, HEAD_DIM))
            if n % 4 == 3:
                yield

    def recurrence_stage():
        nw = nw_ref[...]
        state = jnp.zeros((HEAD_DIM, HEAD_DIM), F32)
        for n in range(n_chunks):
            rows = slice(n * c, (n + 1) * c)
            sb = state.astype(BF16)
            o = _dot(qps[prv, n], sb) + olocs[prv, n]
            o = o * lax.rsqrt(jnp.mean(o * o, axis=-1, keepdims=True) + NORM_EPS)
            o = o * nw * zs_ref[rows, :].astype(F32)
            o_ref[rows, :] = o.astype(BF16)
            state = state * gls[prv, n][0:1, :] + cs[prv, n] - _dot(xs[prv, n], sb)
            yield

    for _ in conv_steps():
        pass

    def prep_groups():
        for g0 in range(0, n_chunks, GDN_GROUP):
            yield from prep_stage(list(range(g0, g0 + GDN_GROUP)))

    return prep_groups(), recurrence_stage()


def _mixers(proj, conv_w, bg, norm_w, batch, seq):
    c = GDN_CHUNK
    n_chunks = seq // c
    n_heads_total = batch * HEADS
    rd = lambda t: jnp.minimum(t, n_heads_total - 1)
    wr = lambda t: jnp.maximum(t - 1, 0)
    blk = lambda off, hd: pl.BlockSpec(
        (seq, HEAD_DIM), lambda t: (hd(t) // HEADS, off + lax.rem(hd(t), HEADS)))
    cw = lambda off: pl.BlockSpec(
        (CONV_WIDTH, HEAD_DIM), lambda t: (0, off + lax.rem(rd(t), HEADS)))
    out = jax.ShapeDtypeStruct((batch * seq, HEADS * HEAD_DIM), BF16)
    return pl.pallas_call(
        _mixers_kernel,
        grid=(n_heads_total + 1,),
        in_specs=[
            blk(QKV_A, rd), blk(QKV_A + HEADS, rd), blk(QKV_A + 2 * HEADS, rd),
            cw(0), cw(HEADS), cw(2 * HEADS),
            pl.BlockSpec((seq, LANES), lambda t: (rd(t) // HEADS, 0)),
            blk(Z_A, wr),
            pl.BlockSpec((1, HEAD_DIM), lambda t: (0, 0)),
            blk(QKV_B, rd), blk(QKV_B + HEADS, rd), blk(QKV_B + 2 * HEADS, rd), blk(Z_B, rd),
        ],
        out_specs=[blk(0, wr), blk(0, rd)],
        out_shape=[out, out],
        scratch_shapes=[
            pltpu.VMEM((seq + SUBLANES, HEAD_DIM), F32),
            pltpu.VMEM((seq, HEAD_DIM), F32),
            pltpu.VMEM((seq, HEAD_DIM), F32),
            pltpu.VMEM((seq, HEAD_DIM), F32),
            pltpu.VMEM((seq, LANES), F32),
            pltpu.VMEM((seq, LANES), F32),
            pltpu.VMEM((2, n_chunks, HEAD_DIM, HEAD_DIM), BF16),
            pltpu.VMEM((2, n_chunks, HEAD_DIM, HEAD_DIM), F32),
            pltpu.VMEM((2, n_chunks, c, HEAD_DIM), BF16),
            pltpu.VMEM((2, n_chunks, c, HEAD_DIM), F32),
            pltpu.VMEM((2, n_chunks, SUBLANES, HEAD_DIM), F32),
        ],
        compiler_params=pltpu.CompilerParams(
            dimension_semantics=("arbitrary",), vmem_limit_bytes=VMEM_LIMIT),
        name="mixers",
    )(proj, proj, proj, conv_w, conv_w, conv_w, bg, proj, norm_w, proj, proj, proj, proj)


def _moba_stages(q_ref, k_ref, v_ref, zs_ref, o_ref):
    s = q_ref.shape[0]
    blk = MOBA_BLOCK
    nb = s // blk

    prow = lax.broadcasted_iota(jnp.int32, (LANES, s), 0)
    pcol = lax.broadcasted_iota(jnp.int32, (LANES, s), 1)
    pind = jnp.where(prow == pcol // blk, 1.0 / blk, 0.0).astype(BF16)
    kmean = _dot(pind, k_ref[...])
    cand = 2 * SUBLANES
    km_full = kmean.astype(BF16)
    km_hi = km_full[0:cand, :]
    km_lo = (kmean - km_full.astype(F32)).astype(BF16)[0:cand, :]

    sub = lax.broadcasted_iota(jnp.int32, (cand, blk), 0)
    key_i = lax.broadcasted_iota(jnp.int32, (blk, blk), 0)
    qry_i = lax.broadcasted_iota(jnp.int32, (blk, blk), 1)
    causal = key_i <= qry_i
    v_t = [v_ref[j * blk:(j + 1) * blk, :].astype(F32).T.astype(BF16) for j in range(nb)]

    def query_block(i):
        rows = slice(i * blk, (i + 1) * blk)
        q = q_ref[rows, :]
        pieces = [_dot_nt(k_ref[j * blk:(j + 1) * blk, :], q) for j in range(i + 1)]
        gt = _dot_nt(km_hi, q) + _dot_nt(km_lo, q) if i > MOBA_TOPK else None
        yield
        if i > MOBA_TOPK:
            rank = jnp.zeros((cand, blk), jnp.int32)
            for jp in range(i):
                gj = jnp.broadcast_to(gt[jp:jp + 1, :], (cand, blk))
                ahead = jnp.logical_or(gj > gt, jnp.logical_and(gj == gt, sub > jp))
                rank = rank + ahead.astype(jnp.int32)
            bias = jnp.where(rank >= MOBA_TOPK, NEG_INF, 0.0)
            for j in range(i):
                pieces[j] = pieces[j] + bias[j:j + 1, :]
        pieces[i] = jnp.where(causal, pieces[i], NEG_INF)
        mt = pieces[0]
        for pc in pieces[1:]:
            mt = jnp.maximum(mt, pc)
        m = jnp.max(mt, axis=0, keepdims=True)
        yield
        pt = None
        o_t = jnp.zeros((HEAD_DIM, blk), F32)
        for j, pc in enumerate(pieces):
            p = jnp.exp2(pc - m)
            pt = p if pt is None else pt + p
            o_t = o_t + _dot(v_t[j], p.astype(BF16))
            if j % 2 == 1:
                yield
        yield
        l = jnp.sum(pt, axis=0, keepdims=True)
        o = (o_t / l).T
        o_ref[rows, :] = (o * zs_ref[rows, :].astype(F32)).astype(BF16)

    order = [b for i in range(nb // 2) for b in (nb - 1 - i, i)]
    return _stagger(*[query_block(i) for i in order])


def _outproj_kernel(oa_ref, ob_ref, ga_ref, gb_ref, x_ref, wa_ref, wb_ref, wo_ref, pw_ref, o_ref):
    tm = o_ref.shape[0]
    halves = [slice(r, r + tm // OUT_SPLIT) for r in range(0, tm, tm // OUT_SPLIT)]
    ua = [_dot(oa_ref[r, :], wa_ref[...]) for r in halves]
    ub = [_dot(ob_ref[r, :], wb_ref[...]) for r in halves]
    merged = [(_sigmoid(ga_ref[r, :].astype(F32)) * a
               + _sigmoid(gb_ref[r, :].astype(F32)) * b).astype(BF16)
              for r, a, b in zip(halves, ua, ub)]
    ys = [_dot(mg, wo_ref[...]) for mg in merged]
    for r, y in zip(halves, ys):
        inv = lax.rsqrt(jnp.mean(y * y, axis=-1, keepdims=True) + NORM_EPS)
        o_ref[r, :] = x_ref[r, :] + y * inv * pw_ref[...]


def _outproj(oa, ob, proj, x2, wa, wb, wo, pw):
    m, d = x2.shape
    width = oa.shape[1]
    tm = OUT_TM
    const = lambda shape: pl.BlockSpec(shape, lambda i: (0, 0), pipeline_mode=pl.Buffered(1))
    return pl.pallas_call(
        _outproj_kernel,
        grid=(m // tm,),
        in_specs=[
            pl.BlockSpec((tm, width), lambda i: (i, 0)),
            pl.BlockSpec((tm, width), lambda i: (i, 0)),
            pl.BlockSpec((tm, d), lambda i: (i, GATE_A_BLK2048)),
            pl.BlockSpec((tm, d), lambda i: (i, GATE_B_BLK2048)),
            pl.BlockSpec((tm, d), lambda i: (i, 0)),
            const((width, d)), const((width, d)), const((d, d)), const((1, d)),
        ],
        out_specs=pl.BlockSpec((tm, d), lambda i: (i, 0)),
        out_shape=jax.ShapeDtypeStruct((m, d), F32),
        compiler_params=pltpu.CompilerParams(
            dimension_semantics=("parallel",), vmem_limit_bytes=VMEM_LIMIT),
        name="outproj",
    )(oa, ob, proj, proj, x2, wa, wb, wo, pw)


def _layer(x, pre_w, w_in, conv_w, a_log, dt_bias, gdn_norm_w, w_a, w_b, w_out, post_w):
    batch, seq, d = x.shape
    width = HEADS * HEAD_DIM
    ba_lo = 4 * width
    ba_hi = ba_lo + 2 * HEADS
    x2 = x.reshape(batch * seq, d)
    w_t = jnp.swapaxes(w_in, 0, 1)
    w_ba = jnp.pad(w_t[ba_lo:ba_hi, :], ((0, LANES - 2 * HEADS), (0, 0)))
    pad_row = lambda v: jnp.pad(v.astype(F32), (HEADS, LANES - 2 * HEADS)).reshape(1, LANES)
    proj, bg = _inproj(x2, pre_w.reshape(1, d), w_t, w_ba, pad_row(a_log), pad_row(dt_bias))
    oa, ob = _mixers(proj, conv_w, bg, gdn_norm_w.reshape(1, HEAD_DIM), batch, seq)
    out = _outproj(oa, ob, proj, x2, w_a.astype(BF16), w_b.astype(BF16), w_out.astype(BF16),
                   post_w.reshape(1, d))
    return out.reshape(batch, seq, d)


def kernel(x, pre_norm_w, w_in, conv_w, a_log, dt_bias, gdn_norm_w, w_branch_a, w_branch_b, w_out,
           post_norm_w):
    depth = pre_norm_w.shape[0]
    for l in range(depth):
        x = _layer(x, pre_norm_w[l], w_in[l], conv_w[l], a_log[l], dt_bias[l], gdn_norm_w[l],
                   w_branch_a[l], w_branch_b[l], w_out[l], post_norm_w[l])
    return x
```

```python
import jax
import jax.numpy as jnp
from jax import lax
from jax.experimental import pallas as pl
from jax.experimental.pallas import tpu as pltpu

F32 = jnp.float32
BF16 = jnp.bfloat16

NORM_EPS = 1e-6
NEG_INF = -1e30

HEADS = 8
HEAD_DIM = 128
CONV_WIDTH = 4
MOBA_BLOCK = 256
MOBA_TOPK = 3
GDN_CHUNK = 128
GDN_GROUP = 16
STAGGER_PERIOD = 1

LANES = 128
SUBLANES = 8

QKV_A, Z_A, QKV_B, Z_B = 0, 24, 32, 56
GATE_A_BLK2048, GATE_B_BLK2048 = 4, 5

INPROJ_TM = 1024
INPROJ_TN = 1024
TILE_BLKS = INPROJ_TN // LANES
N_LO_TILES = (4 * HEADS * HEAD_DIM) // INPROJ_TN
OUT_TM = 512
OUT_SPLIT = 4

MOBA_QSCALE = (HEAD_DIM ** -0.5) * 1.4426950408889634

VMEM_LIMIT = 56 * 1024 * 1024


def _dot(a, b):
    return jnp.dot(a, b, preferred_element_type=F32)


def _dot_nt(a, b):
    return lax.dot_general(a, b, (((1,), (1,)), ((), ())), preferred_element_type=F32)


def _dot_tn(a, b):
    return lax.dot_general(a, b, (((0,), (0,)), ((), ())), preferred_element_type=F32)


def _sigmoid(x):
    return 0.5 * jnp.tanh(0.5 * x) + 0.5


def _silu(x):
    half = 0.5 * x
    return half * (1.0 + jnp.tanh(half))


def _softplus(x):
    return jnp.maximum(x, 0.0) + jnp.log1p(jnp.exp(-jnp.abs(x)))


def _inproj_kernel(x_ref, nw_ref, w_ref, wba_ref, alog_ref, dtb_ref, proj_ref, bg_ref, h_ref):
    j = pl.program_id(1)

    @pl.when(j == 0)
    def _():
        xf = x_ref[...]
        inv = lax.rsqrt(jnp.mean(xf * xf, axis=-1, keepdims=True) + NORM_EPS)
        h_ref[...] = (xf * inv * nw_ref[...]).astype(h_ref.dtype)
        ba = _dot_nt(h_ref[...], wba_ref[...])
        lane = lax.broadcasted_iota(jnp.int32, ba.shape, 1)
        beta = _sigmoid(ba)
        g = -jnp.exp(alog_ref[...]) * _softplus(ba + dtb_ref[...])
        bg_ref[...] = jnp.where(lane < HEADS, beta, g)

    lin = jnp.where(j == QKV_B // TILE_BLKS, MOBA_QSCALE, 1.0)
    proj_ref[...] = (_dot_nt(h_ref[...], w_ref[...]) * lin).astype(BF16)


def _inproj(x2, nw, w_t, w_ba, alog_row, dtb_row):
    m, d = x2.shape
    tm, tn = INPROJ_TM, INPROJ_TN
    n = w_t.shape[0] - 2 * HEADS
    assert n % tn == 0 and N_LO_TILES * tn == 4 * HEADS * HEAD_DIM

    def w_rows(i, j):
        return (pl.multiple_of(j * tn + jnp.where(j >= N_LO_TILES, 2 * HEADS, 0), 2 * HEADS), 0)

    return pl.pallas_call(
        _inproj_kernel,
        grid=(m // tm, n // tn),
        in_specs=[
            pl.BlockSpec((tm, d), lambda i, j: (i, 0)),
            pl.BlockSpec((1, d), lambda i, j: (0, 0)),
            pl.BlockSpec((pl.Element(tn), pl.Element(d)), w_rows),
            pl.BlockSpec((LANES, d), lambda i, j: (0, 0)),
            pl.BlockSpec((1, LANES), lambda i, j: (0, 0)),
            pl.BlockSpec((1, LANES), lambda i, j: (0, 0)),
        ],
        out_specs=[
            pl.BlockSpec((tm, tn), lambda i, j: (i, j)),
            pl.BlockSpec((tm, LANES), lambda i, j: (i, 0)),
        ],
        out_shape=[
            jax.ShapeDtypeStruct((m, n), BF16),
            jax.ShapeDtypeStruct((m, LANES), F32),
        ],
        scratch_shapes=[pltpu.VMEM((tm, d), BF16)],
        compiler_params=pltpu.CompilerParams(
            dimension_semantics=("parallel", "arbitrary"), vmem_limit_bytes=VMEM_LIMIT),
        name="inproj",
    )(x2, nw, w_t, w_ba, alog_row, dtb_row)


def _interleave(*stages):
    live = list(stages)
    while live:
        for g in list(live):
            try:
                next(g)
            except StopIteration:
                live.remove(g)


def _stagger(*stages):
    live = []
    pending = list(stages)
    rnd = 0
    while live or pending:
        if pending and rnd % STAGGER_PERIOD == 0:
            live.append(pending.pop(0))
        rnd += 1
        for g in list(live):
            try:
                next(g)
            except StopIteration:
                live.remove(g)
        yield


def _tri_inverse_group(lmats, row, col, out):
    c = lmats[0].shape[0]

    def sub_mask(s_log):
        bi = jnp.right_shift(row, s_log)
        bj = jnp.right_shift(col, s_log)
        return jnp.logical_and(jnp.bitwise_xor(bi, bj) == 1, jnp.bitwise_and(bi, 1) == 1)

    eye = (row == col).astype(BF16)
    lbs = [lm.astype(BF16) for lm in lmats]
    m0 = sub_mask(0).astype(BF16)
    tds = [eye - lb * m0 for lb in lbs]
    s_log = 1
    while (1 << s_log) < c:
        ms = sub_mask(s_log).astype(BF16)
        x1s = [_dot(td, lb * ms).astype(BF16) for td, lb in zip(tds, lbs)]
        yield
        tds = [td - _dot(x1, td).astype(BF16) for td, x1 in zip(tds, x1s)]
        yield
        s_log += 1
    out.extend(tds)


def _mixers_kernel(q_ref, k_ref, v_ref, cwq_ref, cwk_ref, cwv_ref, bg_ref, zs_ref, nw_ref,
                   mq_ref, mk_ref, mv_ref, mzs_ref, o_ref, mo_ref,
                   xpad, qn, kn, vn, betab, gb, xs, cs, qps, olocs, gls):
    t = pl.program_id(0)

    @pl.when(t == 0)
    def _():
        xs[1] = jnp.zeros(xs.shape[1:], xs.dtype)
        cs[1] = jnp.zeros(cs.shape[1:], cs.dtype)
        qps[1] = jnp.zeros(qps.shape[1:], qps.dtype)
        olocs[1] = jnp.zeros(olocs.shape[1:], olocs.dtype)
        gls[1] = jnp.zeros(gls.shape[1:], gls.dtype)

    gdn_stages = _gdn_stages(q_ref, k_ref, v_ref, cwq_ref, cwk_ref, cwv_ref, bg_ref, zs_ref, nw_ref,
                             o_ref, xpad, qn, kn, vn, betab, gb, xs, cs, qps, olocs, gls)
    moba_stage = _moba_stages(mq_ref, mk_ref, mv_ref, mzs_ref, mo_ref)
    _interleave(*gdn_stages, moba_stage)


def _gdn_stages(q_ref, k_ref, v_ref, cwq_ref, cwk_ref, cwv_ref, bg_ref, zs_ref, nw_ref, o_ref,
                xpad, qn, kn, vn, betab, gb, xs, cs, qps, olocs, gls):
    s = q_ref.shape[0]
    c = GDN_CHUNK
    n_chunks = s // c
    t = pl.program_id(0)
    n_heads_total = pl.num_programs(0) - 1
    h = lax.rem(jnp.minimum(t, n_heads_total - 1), HEADS)
    cur = lax.rem(t, 2)
    prv = 1 - cur

    def conv_silu(src_ref, cw_ref):
        xpad[0:SUBLANES, :] = jnp.zeros((SUBLANES, HEAD_DIM), F32)
        xpad[SUBLANES:SUBLANES + s, :] = src_ref[...].astype(F32)
        base = SUBLANES - (CONV_WIDTH - 1)
        acc = cw_ref[0:1, :] * xpad[base:base + s, :]
        for j in range(1, CONV_WIDTH):
            acc = acc + cw_ref[j:j + 1, :] * xpad[base + j:base + j + s, :]
        half = 0.5 * acc
        return half * (1.0 + jnp.tanh(half))

    def l2n(x, scale=1.0):
        return x * (lax.rsqrt(jnp.sum(x * x, axis=-1, keepdims=True) + NORM_EPS) * scale)

    def conv_steps():
        qn[...] = l2n(conv_silu(q_ref, cwq_ref), HEAD_DIM ** -0.5)
        yield
        kn[...] = l2n(conv_silu(k_ref, cwk_ref))
        yield
        vn[...] = conv_silu(v_ref, cwv_ref)
        yield
        bg = bg_ref[...]
        lane = lax.broadcasted_iota(jnp.int32, bg.shape, 1)
        beta_col = jnp.sum(jnp.where(lane == h, bg, 0.0), axis=-1, keepdims=True)
        g_col = jnp.sum(jnp.where(lane == h + HEADS, bg, 0.0), axis=-1, keepdims=True)
        betab[...] = jnp.broadcast_to(beta_col, bg.shape)
        gb[...] = jnp.broadcast_to(g_col, bg.shape)
        yield

    row = lax.broadcasted_iota(jnp.int32, (c, c), 0)
    col = lax.broadcasted_iota(jnp.int32, (c, c), 1)
    incl = row >= col
    strict = row > col
    tril_b = incl.astype(BF16)

    def cumsum_rows(x):
        x1 = x.astype(BF16)
        r1 = x - x1.astype(F32)
        x2 = r1.astype(BF16)
        x3 = (r1 - x2.astype(F32)).astype(BF16)
        return _dot(tril_b, x1) + _dot(tril_b, x2) + _dot(tril_b, x3)

    def prep_stage(ns):
        rows = [slice(n * c, (n + 1) * c) for n in ns]
        lane = lax.broadcasted_iota(jnp.int32, (c, LANES), 1)
        gcol = jnp.zeros((c, LANES), F32)
        for n, r in zip(ns, rows):
            gcol = jnp.where(lane == n, gb[r, :], gcol)
        csum = cumsum_rows(gcol)
        csum_t = csum.T
        yield
        gcs = [jnp.broadcast_to(csum[:, n:n + 1], (c, c)) for n in ns]
        decays = [jnp.exp(jnp.where(incl, gc - jnp.broadcast_to(csum_t[n:n + 1, :], (c, c)),
                                    NEG_INF)) for n, gc in zip(ns, gcs)]
        a2s = [_dot_nt(jnp.concatenate([kn[r, :] * betab[r, :], qn[r, :]], axis=0).astype(BF16),
                       kn[r, :].astype(BF16)) for r in rows]
        yield
        lmats = [jnp.where(strict, a2[:c] * d, 0.0) for a2, d in zip(a2s, decays)]
        ts = []
        yield from _tri_inverse_group(lmats, row, col, ts)
        uwbs = []
        for r, gc, tinv in zip(rows, gcs, ts):
            beta = betab[r, :]
            rhs = jnp.concatenate([kn[r, :] * beta * jnp.exp(gc), vn[r, :] * beta], axis=1)
            uwbs.append(_dot(tinv.astype(BF16), rhs.astype(BF16)).astype(BF16))
        yield
        for n, r, gc, a2, d, uwb in zip(ns, rows, gcs, a2s, decays, uwbs):
            g_last = gc[c - 1:c, :]
            intra = jnp.where(incl, a2[c:] * d, 0.0)
            kdec = (kn[r, :] * jnp.exp(g_last - gc)).astype(BF16)
            kwu = _dot_tn(kdec, uwb)
            iwu = _dot(intra.astype(BF16), uwb)
            xs[cur, n] = kwu[:, :HEAD_DIM].astype(BF16)
            cs[cur, n] = kwu[:, HEAD_DIM:]
            qps[cur, n] = (qn[r, :] * jnp.exp(gc) - iwu[:, :HEAD_DIM]).astype(BF16)
            olocs[cur, n] = iwu[:, HEAD_DIM:]
            gls[cur, n] = jnp.broadcast_to(jnp.exp(g_last), (SUBLANES, HEAD_DIM))
            if n % 4 == 3:
                yield

    def recurrence_stage():
        nw = nw_ref[...]
        state = jnp.zeros((HEAD_DIM, HEAD_DIM), F32)
        for n in range(n_chunks):
            rows = slice(n * c, (n + 1) * c)
            sb = state.astype(BF16)
            o = _dot(qps[prv, n], sb) + olocs[prv, n]
            o = o * lax.rsqrt(jnp.mean(o * o, axis=-1, keepdims=True) + NORM_EPS)
            o = o * nw * _silu(zs_ref[rows, :].astype(F32))
            o_ref[rows, :] = o.astype(BF16)
            state = state * gls[prv, n][0:1, :] + cs[prv, n] - _dot(xs[prv, n], sb)
            yield

    for _ in conv_steps():
        pass

    def prep_groups():
        for g0 in range(0, n_chunks, GDN_GROUP):
            yield from prep_stage(list(range(g0, g0 + GDN_GROUP)))

    return prep_groups(), recurrence_stage()


def _mixers(proj, conv_w, bg, norm_w, batch, seq):
    c = GDN_CHUNK
    n_chunks = seq // c
    n_heads_total = batch * HEADS
    rd = lambda t: jnp.minimum(t, n_heads_total - 1)
    wr = lambda t: jnp.maximum(t - 1, 0)
    blk = lambda off, hd: pl.BlockSpec(
        (seq, HEAD_DIM), lambda t: (hd(t) // HEADS, off + lax.rem(hd(t), HEADS)))
    cw = lambda off: pl.BlockSpec(
        (CONV_WIDTH, HEAD_DIM), lambda t: (0, off + lax.rem(rd(t), HEADS)))
    out = jax.ShapeDtypeStruct((batch * seq, HEADS * HEAD_DIM), BF16)
    return pl.pallas_call(
        _mixers_kernel,
        grid=(n_heads_total + 1,),
        in_specs=[
            blk(QKV_A, rd), blk(QKV_A + HEADS, rd), blk(QKV_A + 2 * HEADS, rd),
            cw(0), cw(HEADS), cw(2 * HEADS),
            pl.BlockSpec((seq, LANES), lambda t: (rd(t) // HEADS, 0)),
            blk(Z_A, wr),
            pl.BlockSpec((1, HEAD_DIM), lambda t: (0, 0)),
            blk(QKV_B, rd), blk(QKV_B + HEADS, rd), blk(QKV_B + 2 * HEADS, rd), blk(Z_B, rd),
        ],
        out_specs=[blk(0, wr), blk(0, rd)],
        out_shape=[out, out],
        scratch_shapes=[
            pltpu.VMEM((seq + SUBLANES, HEAD_DIM), F32),
            pltpu.VMEM((seq, HEAD_DIM), F32),
            pltpu.VMEM((seq, HEAD_DIM), F32),
            pltpu.VMEM((seq, HEAD_DIM), F32),
            pltpu.VMEM((seq, LANES), F32),
            pltpu.VMEM((seq, LANES), F32),
            pltpu.VMEM((2, n_chunks, HEAD_DIM, HEAD_DIM), BF16),
            pltpu.VMEM((2, n_chunks, HEAD_DIM, HEAD_DIM), F32),
            pltpu.VMEM((2, n_chunks, c, HEAD_DIM), BF16),
            pltpu.VMEM((2, n_chunks, c, HEAD_DIM), F32),
            pltpu.VMEM((2, n_chunks, SUBLANES, HEAD_DIM), F32),
        ],
        compiler_params=pltpu.CompilerParams(
            dimension_semantics=("arbitrary",), vmem_limit_bytes=VMEM_LIMIT),
        name="mixers",
    )(proj, proj, proj, conv_w, conv_w, conv_w, bg, proj, norm_w, proj, proj, proj, proj)


def _moba_stages(q_ref, k_ref, v_ref, zs_ref, o_ref):
    s = q_ref.shape[0]
    blk = MOBA_BLOCK
    nb = s // blk

    prow = lax.broadcasted_iota(jnp.int32, (LANES, s), 0)
    pcol = lax.broadcasted_iota(jnp.int32, (LANES, s), 1)
    pind = jnp.where(prow == pcol // blk, 1.0 / blk, 0.0).astype(BF16)
    kmean = _dot(pind, k_ref[...])
    cand = 2 * SUBLANES
    km_full = kmean.astype(BF16)
    km_hi = km_full[0:cand, :]
    km_lo = (kmean - km_full.astype(F32)).astype(BF16)[0:cand, :]

    sub = lax.broadcasted_iota(jnp.int32, (cand, blk), 0)
    key_i = lax.broadcasted_iota(jnp.int32, (blk, blk), 0)
    qry_i = lax.broadcasted_iota(jnp.int32, (blk, blk), 1)
    causal = key_i <= qry_i
    v_t = [v_ref[j * blk:(j + 1) * blk, :].astype(F32).T.astype(BF16) for j in range(nb)]

    def query_block(i):
        rows = slice(i * blk, (i + 1) * blk)
        q = q_ref[rows, :]
        pieces = [_dot_nt(k_ref[j * blk:(j + 1) * blk, :], q) for j in range(i + 1)]
        gt = _dot_nt(km_hi, q) + _dot_nt(km_lo, q) if i > MOBA_TOPK else None
        yield
        if i > MOBA_TOPK:
            rank = jnp.zeros((cand, blk), jnp.int32)
            for jp in range(i):
                gj = jnp.broadcast_to(gt[jp:jp + 1, :], (cand, blk))
                ahead = jnp.logical_or(gj > gt, jnp.logical_and(gj == gt, sub > jp))
                rank = rank + ahead.astype(jnp.int32)
            bias = jnp.where(rank >= MOBA_TOPK, NEG_INF, 0.0)
            for j in range(i):
                pieces[j] = pieces[j] + bias[j:j + 1, :]
        pieces[i] = jnp.where(causal, pieces[i], NEG_INF)
        mt = pieces[0]
        for pc in pieces[1:]:
            mt = jnp.maximum(mt, pc)
        m = jnp.max(mt, axis=0, keepdims=True)
        yield
        pt = None
        o_t = jnp.zeros((HEAD_DIM, blk), F32)
        for j, pc in enumerate(pieces):
            p = jnp.exp2(pc - m)
            pt = p if pt is None else pt + p
            o_t = o_t + _dot(v_t[j], p.astype(BF16))
            if j % 2 == 1:
                yield
        yield
        l = jnp.sum(pt, axis=0, keepdims=True)
        o = (o_t / l).T
        o_ref[rows, :] = (o * _silu(zs_ref[rows, :].astype(F32))).astype(BF16)

    order = [b for i in range(nb // 2) for b in (nb - 1 - i, i)]
    return _stagger(*[query_block(i) for i in order])


def _outproj_kernel(oa_ref, ob_ref, ga_ref, gb_ref, x_ref, wa_ref, wb_ref, wo_ref, pw_ref, o_ref):
    tm = o_ref.shape[0]
    halves = [slice(r, r + tm // OUT_SPLIT) for r in range(0, tm, tm // OUT_SPLIT)]
    ua = [_dot(oa_ref[r, :], wa_ref[...]) for r in halves]
    ub = [_dot(ob_ref[r, :], wb_ref[...]) for r in halves]
    merged = [(_sigmoid(ga_ref[r, :].astype(F32)) * a
               + _sigmoid(gb_ref[r, :].astype(F32)) * b).astype(BF16)
              for r, a, b in zip(halves, ua, ub)]
    ys = [_dot(mg, wo_ref[...]) for mg in merged]
    for r, y in zip(halves, ys):
        inv = lax.rsqrt(jnp.mean(y * y, axis=-1, keepdims=True) + NORM_EPS)
        o_ref[r, :] = x_ref[r, :] + y * inv * pw_ref[...]


def _outproj(oa, ob, proj, x2, wa, wb, wo, pw):
    m, d = x2.shape
    width = oa.shape[1]
    tm = OUT_TM
    const = lambda shape: pl.BlockSpec(shape, lambda i: (0, 0), pipeline_mode=pl.Buffered(1))
    return pl.pallas_call(
        _outproj_kernel,
        grid=(m // tm,),
        in_specs=[
            pl.BlockSpec((tm, width), lambda i: (i, 0)),
            pl.BlockSpec((tm, width), lambda i: (i, 0)),
            pl.BlockSpec((tm, d), lambda i: (i, GATE_A_BLK2048)),
            pl.BlockSpec((tm, d), lambda i: (i, GATE_B_BLK2048)),
            pl.BlockSpec((tm, d), lambda i: (i, 0)),
            const((width, d)), const((width, d)), const((d, d)), const((1, d)),
        ],
        out_specs=pl.BlockSpec((tm, d), lambda i: (i, 0)),
        out_shape=jax.ShapeDtypeStruct((m, d), F32),
        compiler_params=pltpu.CompilerParams(
            dimension_semantics=("parallel",), vmem_limit_bytes=VMEM_LIMIT),
        name="outproj",
    )(oa, ob, proj, proj, x2, wa, wb, wo, pw)


def _layer(x, pre_w, w_in, conv_w, a_log, dt_bias, gdn_norm_w, w_a, w_b, w_out, post_w):
    batch, seq, d = x.shape
    width = HEADS * HEAD_DIM
    ba_lo = 4 * width
    ba_hi = ba_lo + 2 * HEADS
    x2 = x.reshape(batch * seq, d)
    w_t = jnp.swapaxes(w_in, 0, 1)
    w_ba = jnp.pad(w_t[ba_lo:ba_hi, :], ((0, LANES - 2 * HEADS), (0, 0)))
    pad_row = lambda v: jnp.pad(v.astype(F32), (HEADS, LANES - 2 * HEADS)).reshape(1, LANES)
    proj, bg = _inproj(x2, pre_w.reshape(1, d), w_t, w_ba, pad_row(a_log), pad_row(dt_bias))
    oa, ob = _mixers(proj, conv_w, bg, gdn_norm_w.reshape(1, HEAD_DIM), batch, seq)
    out = _outproj(oa, ob, proj, x2, w_a.astype(BF16), w_b.astype(BF16), w_out.astype(BF16),
                   post_w.reshape(1, d))
    return out.reshape(batch, seq, d)


def kernel(x, pre_norm_w, w_in, conv_w, a_log, dt_bias, gdn_norm_w, w_branch_a, w_branch_b, w_out,
           post_norm_w):
    depth = pre_norm_w.shape[0]
    for l in range(depth):
        x = _layer(x, pre_norm_w[l], w_in[l], conv_w[l], a_log[l], dt_bias[l], gdn_norm_w[l],
                   w_branch_a[l], w_branch_b[l], w_out[l], post_norm_w[l])
    return x
```
